```python
import math
import jax, jax.numpy as jnp
from jax import lax
import numpy as np

D_MODEL = 1024
BATCH = 4
SEQ = 8192
DEPTH = 2

GLA_HEADS = D_MODEL // 256
GLA_DK = 32
GLA_DV = 64
GLA_GATE_RANK = 16
GLA_TAU = 16.0
GLA_CHUNK = 16
NSA_HEADS = D_MODEL // 256
NSA_KV_HEADS = NSA_HEADS // 2
NSA_DH = 64
NSA_CMP_LEN = 32
NSA_CMP_STRIDE = 16
NSA_CMP_HIDDEN = 128
NSA_SEL_BLOCK = 64
NSA_SEL_TOPK = 16
NSA_SEL_LOCAL = 2
NSA_WINDOW = 512
NSA_QBLOCK = 128
SSD_HEADS = D_MODEL // 128
SSD_HEAD_DIM = 64
SSD_GROUPS = 2
SSD_STATE = 128
SSD_CONV = 4
SSD_CHUNK = 128
SSD_D_INNER = SSD_HEADS * SSD_HEAD_DIM
SSD_CONV_DIM = SSD_D_INNER + 2 * SSD_GROUPS * SSD_STATE

D_MIX = GLA_HEADS * GLA_DV + NSA_HEADS * NSA_DH + SSD_D_INNER
D_FF = 4 * D_MODEL
ROPE_THETA = 10000.0
NORM_EPS = 1e-6
NEG = -1e30
BIG = 1e30

IN_SPLITS = (
    GLA_HEADS * GLA_DK,
    GLA_HEADS * GLA_DK,
    GLA_HEADS * GLA_DV,
    GLA_GATE_RANK,
    GLA_HEADS * GLA_DV,
    NSA_HEADS * NSA_DH,
    6 * NSA_KV_HEADS * NSA_DH,
    3 * NSA_HEADS,
    SSD_D_INNER,
    SSD_CONV_DIM,
    SSD_HEADS,
)
D_IN = sum(IN_SPLITS)

kernel_name = "hybrid_gla_nsa_ssd_block"


def rms_norm(x, w):
    xf = x.astype(jnp.float32)
    y = xf * lax.rsqrt(jnp.mean(xf * xf, axis=-1, keepdims=True) + NORM_EPS)
    return (y * w.astype(jnp.float32)).astype(x.dtype)


def rope(x, pos):
    half = x.shape[-1] // 2
    inv = ROPE_THETA ** (-jnp.arange(half, dtype=jnp.float32) / half)
    ang = pos.astype(jnp.float32)[:, None] * inv[None, :]
    cos = jnp.cos(ang)[None, :, None, :]
    sin = jnp.sin(ang)[None, :, None, :]
    x1 = x[..., :half].astype(jnp.float32)
    x2 = x[..., half:].astype(jnp.float32)
    return jnp.concatenate([x1 * cos - x2 * sin, x2 * cos + x1 * sin], axis=-1).astype(x.dtype)


def gla_mixer(q, k, v, g_lr, r, w_gate2, b_gate, norm_w):
    f32 = jnp.float32
    Bsz, S, _ = q.shape
    H, dk, dv, C = GLA_HEADS, GLA_DK, GLA_DV, GLA_CHUNK
    nc = S // C
    log_a = jax.nn.log_sigmoid((g_lr @ w_gate2 + b_gate).astype(f32)) / GLA_TAU

    def chunks(t, d):
        return t.astype(f32).reshape(Bsz, nc, C, H, d).transpose(0, 3, 1, 2, 4)

    qc = chunks(q, dk) * (dk ** -0.5)
    kc = chunks(k, dk)
    vc = chunks(v, dv)
    bc = jnp.cumsum(chunks(log_a, dk), axis=3)
    causal = jnp.tril(jnp.ones((C, C), dtype=bool))
    diff = bc[..., :, None, :] - bc[..., None, :, :]
    decay = jnp.exp(jnp.where(causal[..., None], diff, -jnp.inf))
    attn = jnp.einsum('bhnid,bhnjd,bhnijd->bhnij', qc, kc, decay)
    o_intra = jnp.einsum('bhnij,bhnjd->bhnid', attn, vc)
    g_tot = bc[..., -1, :]
    u = jnp.einsum('bhnck,bhncv->bhnkv', kc * jnp.exp(g_tot[..., None, :] - bc), vc)

    def step(state, inp):
        g, uc = inp
        return jnp.exp(g)[..., None] * state + uc, state

    init = jnp.zeros((Bsz, H, dk, dv), f32)
    _, s_enter = lax.scan(step, init, (jnp.moveaxis(g_tot, 2, 0), jnp.moveaxis(u, 2, 0)))
    s_enter = jnp.moveaxis(s_enter, 0, 2)
    o_inter = jnp.einsum('bhnck,bhnkv->bhncv', qc * jnp.exp(bc), s_enter)
    o = (o_intra + o_inter).transpose(0, 2, 3, 1, 4).reshape(Bsz, S, H, dv)
    o = rms_norm(o, norm_w.reshape(H, dv)).reshape(Bsz, S, H * dv)
    return (o * jax.nn.silu(r.astype(f32))).astype(q.dtype)


def nsa_compress(t, pos_emb, w1, w2):
    Bsz, S, G, dh = t.shape
    nsub = NSA_CMP_LEN // NSA_CMP_STRIDE
    sub = t.reshape(Bsz, S // NSA_CMP_STRIDE, NSA_CMP_STRIDE, G, dh)
    n_cmp = S // NSA_CMP_STRIDE - nsub + 1
    blocks = jnp.concatenate([sub[:, i:i + n_cmp] for i in range(nsub)], axis=2)
    blocks = blocks + pos_emb[None, None, :, None, :]
    flat = blocks.transpose(0, 3, 1, 2, 4).reshape(Bsz, G, n_cmp, NSA_CMP_LEN * dh)
    return jax.nn.gelu(flat @ w1) @ w2


def nsa_mixer(q, kv, gates, pos, cmp_pos_k, cmp_w1_k, cmp_w2_k,
              cmp_pos_v, cmp_w1_v, cmp_w2_v, norm_w):
    f32 = jnp.float32
    Bsz, S, _ = q.shape
    H, G, dh = NSA_HEADS, NSA_KV_HEADS, NSA_DH
    R = H // G
    QB = NSA_QBLOCK
    out_dtype = q.dtype
    t_idx = jnp.arange(S)

    q = rope(q.reshape(Bsz, S, H, dh), pos)
    k_cmp, v_cmp, k_slc, v_slc, k_win, v_win = [
        t.reshape(Bsz, S, G, dh) for t in jnp.split(kv, 6, axis=-1)]
    k_cmp, k_slc, k_win = rope(k_cmp, pos), rope(k_slc, pos), rope(k_win, pos)
    qg = q.reshape(Bsz, S, G, R, dh).transpose(0, 2, 3, 1, 4).astype(f32) * (dh ** -0.5)

    kc = nsa_compress(k_cmp, cmp_pos_k, cmp_w1_k, cmp_w2_k).astype(f32)
    vc = nsa_compress(v_cmp, cmp_pos_v, cmp_w1_v, cmp_w2_v).astype(f32)
    n_cmp = kc.shape[2]
    cmp_start = jnp.arange(n_cmp) * NSA_CMP_STRIDE
    cmp_mask = (cmp_start + NSA_CMP_LEN - 1)[None, :] <= t_idx[:, None]
    s_cmp = jnp.einsum('bgrsd,bgnd->bgrsn', qg, kc)
    p_cmp = jax.nn.softmax(jnp.where(cmp_mask, s_cmp, NEG), axis=-1)
    p_cmp = jnp.where(cmp_mask, p_cmp, 0.0)
    o_cmp = jnp.einsum('bgrsn,bgnd->bgrsd', p_cmp, vc)

    n_slc = S // NSA_SEL_BLOCK
    slc_start = jnp.arange(n_slc) * NSA_SEL_BLOCK
    overlap = ((cmp_start[:, None] < slc_start[None, :] + NSA_SEL_BLOCK)
               & (cmp_start[:, None] + NSA_CMP_LEN > slc_start[None, :])).astype(f32)
    imp = jnp.einsum('bgrsn,nj->bgsj', p_cmp, overlap)
    q_blk = t_idx // NSA_SEL_BLOCK
    j = jnp.arange(n_slc)
    forced = (j[None, :] == 0) | ((j[None, :] <= q_blk[:, None])
                                  & (j[None, :] > q_blk[:, None] - NSA_SEL_LOCAL))
    future = j[None, :] > q_blk[:, None]
    imp = jnp.where(forced, BIG, jnp.where(future, NEG, imp))
    n_top = min(NSA_SEL_TOPK, n_slc)
    top_val, top_idx = lax.top_k(imp, n_top)
    top_ok = top_val > 0.5 * NEG

    kb = k_slc.astype(f32).transpose(0, 2, 1, 3).reshape(Bsz, G, n_slc, NSA_SEL_BLOCK, dh)
    vb = v_slc.astype(f32).transpose(0, 2, 1, 3).reshape(Bsz, G, n_slc, NSA_SEL_BLOCK, dh)
    nq = S // QB
    qs = jnp.moveaxis(qg.reshape(Bsz, G, R, nq, QB, dh), 3, 0)
    idx_s = jnp.moveaxis(top_idx.reshape(Bsz, G, nq, QB, n_top), 2, 0)
    ok_s = jnp.moveaxis(top_ok.reshape(Bsz, G, nq, QB, n_top), 2, 0)
    tq = t_idx.reshape(nq, QB)
    bi = jnp.arange(Bsz)[:, None, None, None]
    gi = jnp.arange(G)[None, :, None, None]
    offs = jnp.arange(NSA_SEL_BLOCK)

    def sel_block(args):
        qb, ib, okb, tb = args
        kg = kb[bi, gi, ib]
        vg = vb[bi, gi, ib]
        kpos = ib[..., None] * NSA_SEL_BLOCK + offs
        m = okb[..., None] & (kpos <= tb[None, None, :, None, None])
        s = jnp.einsum('bgrqd,bgqnkd->bgrqnk', qb, kg)
        s = jnp.where(m[:, :, None], s, NEG)
        p = jax.nn.softmax(s.reshape(Bsz, G, R, QB, -1), axis=-1).reshape(s.shape)
        return jnp.einsum('bgrqnk,bgqnkd->bgrqd', p, vg)

    o_slc = lax.map(sel_block, (qs, idx_s, ok_s, tq))
    o_slc = jnp.moveaxis(o_slc, 0, 3).reshape(Bsz, G, R, S, dh)

    pad = NSA_WINDOW // QB

    def band(t):
        tb_ = t.astype(f32).transpose(0, 2, 1, 3).reshape(Bsz, G, nq, QB, dh)
        tb_ = jnp.pad(tb_, ((0, 0), (0, 0), (pad, 0), (0, 0), (0, 0)))
        return jnp.concatenate([tb_[:, :, i:i + nq] for i in range(pad + 1)], axis=3)

    kw, vw = band(k_win), band(v_win)
    qw = qg.reshape(Bsz, G, R, nq, QB, dh)
    kpos_w = (jnp.arange(nq)[:, None] - pad) * QB + jnp.arange((pad + 1) * QB)[None, :]
    dist = tq[:, :, None] - kpos_w[:, None, :]
    wmask = (kpos_w[:, None, :] >= 0) & (dist >= 0) & (dist < NSA_WINDOW)
    s_w = jnp.einsum('bgrwqd,bgwkd->bgrwqk', qw, kw)
    p_w = jax.nn.softmax(jnp.where(wmask, s_w, NEG), axis=-1)
    o_win = jnp.einsum('bgrwqk,bgwkd->bgrwqd', p_w, vw).reshape(Bsz, G, R, S, dh)

    gt = jax.nn.sigmoid(gates.astype(f32)).reshape(Bsz, S, G, R, 3).transpose(0, 2, 3, 1, 4)
    o = gt[..., 0:1] * o_cmp + gt[..., 1:2] * o_slc + gt[..., 2:3] * o_win
    o = o.transpose(0, 3, 1, 2, 4).reshape(Bsz, S, H, dh)
    o = rms_norm(o, norm_w.reshape(H, dh))
    return o.reshape(Bsz, S, H * dh).astype(out_dtype)


def ssd_mixer(z, xbc, dt, conv_w, conv_b, dt_bias, a_log, d_skip, norm_w):
    f32 = jnp.float32
    Bsz, S, _ = z.shape
    H, P, G, N, L = SSD_HEADS, SSD_HEAD_DIM, SSD_GROUPS, SSD_STATE, SSD_CHUNK
    nc = S // L
    xbc = lax.conv_general_dilated(
        xbc, conv_w[:, None, :].astype(xbc.dtype), window_strides=(1,),
        padding=[(SSD_CONV - 1, 0)], dimension_numbers=('NWC', 'WIO', 'NWC'),
        feature_group_count=SSD_CONV_DIM) + conv_b
    xbc = jax.nn.silu(xbc.astype(f32))
    xs, Bm, Cm = jnp.split(xbc, [SSD_D_INNER, SSD_D_INNER + G * N], axis=-1)
    xs = xs.reshape(Bsz, S, H, P)
    hpg = H // G
    Bm = jnp.repeat(Bm.reshape(Bsz, S, G, N), hpg, axis=2)
    Cm = jnp.repeat(Cm.reshape(Bsz, S, G, N), hpg, axis=2)
    dt = jax.nn.softplus(dt.astype(f32) + dt_bias.astype(f32))
    A = -jnp.exp(a_log.astype(f32))
    xdt = (xs * dt[..., None]).reshape(Bsz, nc, L, H, P)
    Bc = Bm.reshape(Bsz, nc, L, H, N)
    Cc = Cm.reshape(Bsz, nc, L, H, N)
    acs = jnp.cumsum((dt * A).reshape(Bsz, nc, L, H).transpose(0, 3, 1, 2), axis=-1)
    causal = jnp.tril(jnp.ones((L, L), dtype=bool))
    seg = jnp.exp(jnp.where(causal, acs[..., :, None] - acs[..., None, :], -jnp.inf))
    cb = jnp.einsum('bclhn,bcshn->bhcls', Cc, Bc)
    y_diag = jnp.einsum('bhcls,bcshp->bclhp', cb * seg, xdt)
    decay_states = jnp.exp(acs[..., -1:] - acs)
    states = jnp.einsum('bclhn,bhcl,bclhp->bchpn', Bc, decay_states, xdt)
    chunk_decay = jnp.exp(acs[..., -1])

    def step(h, inp):
        dec, st = inp
        return dec[..., None, None] * h + st, h

    init = jnp.zeros((Bsz, H, P, N), f32)
    _, h_enter = lax.scan(step, init, (jnp.moveaxis(chunk_decay, 2, 0), jnp.moveaxis(states, 1, 0)))
    h_enter = jnp.moveaxis(h_enter, 0, 1)
    y_off = jnp.einsum('bclhn,bchpn,bhcl->bclhp', Cc, h_enter, jnp.exp(acs))
    y = (y_diag + y_off).reshape(Bsz, S, H, P) + d_skip.astype(f32)[:, None] * xs
    y = y.reshape(Bsz, S, H * P) * jax.nn.silu(z.astype(f32))
    y = rms_norm(y.reshape(Bsz, S, G, H * P // G), norm_w.reshape(G, -1)).reshape(Bsz, S, H * P)
    return y.astype(z.dtype)


def setup_inputs(seed: int = 0) -> dict:
    key = jax.random.key(seed)
    ks = iter(jax.random.split(key, 32))
    f32 = jnp.float32
    Ld = DEPTH

    def nrm(shape, scale):
        return jax.random.normal(next(ks), shape, f32) * scale

    def gain(shape):
        return 1.0 + nrm(shape, 0.02)

    cmp_in = NSA_CMP_LEN * NSA_DH
    x = jax.random.normal(next(ks), (BATCH, SEQ, D_MODEL), f32)
    dt0 = jnp.exp(jax.random.uniform(next(ks), (Ld, SSD_HEADS), f32,
                                     math.log(1e-3), math.log(1e-1)))
    dt_bias = dt0 + jnp.log(-jnp.expm1(-dt0))
    a_log = jnp.log(jax.random.uniform(next(ks), (Ld, SSD_HEADS), f32, 1.0, 16.0))
    return {
        "x": x,
        "norm1_w": gain((Ld, D_MODEL)),
        "w_in": nrm((Ld, D_MODEL, D_IN), D_MODEL ** -0.5),
        "gla_gate_w2": nrm((Ld, GLA_GATE_RANK, GLA_HEADS * GLA_DK), GLA_GATE_RANK ** -0.5),
        "gla_gate_b": nrm((Ld, GLA_HEADS * GLA_DK), 0.1),
        "gla_norm_w": gain((Ld, GLA_HEADS * GLA_DV)),
        "nsa_cmp_pos_k": nrm((Ld, NSA_CMP_LEN, NSA_DH), 0.1),
        "nsa_cmp_w1_k": nrm((Ld, cmp_in, NSA_CMP_HIDDEN), cmp_in ** -0.5),
        "nsa_cmp_w2_k": nrm((Ld, NSA_CMP_HIDDEN, NSA_DH), NSA_CMP_HIDDEN ** -0.5),
        "nsa_cmp_pos_v": nrm((Ld, NSA_CMP_LEN, NSA_DH), 0.1),
        "nsa_cmp_w1_v": nrm((Ld, cmp_in, NSA_CMP_HIDDEN), cmp_in ** -0.5),
        "nsa_cmp_w2_v": nrm((Ld, NSA_CMP_HIDDEN, NSA_DH), NSA_CMP_HIDDEN ** -0.5),
        "nsa_norm_w": gain((Ld, NSA_HEADS * NSA_DH)),
        "ssd_conv_w": nrm((Ld, SSD_CONV, SSD_CONV_DIM), SSD_CONV ** -0.5),
        "ssd_conv_b": nrm((Ld, SSD_CONV_DIM), 0.02),
        "ssd_dt_bias": dt_bias,
        "ssd_a_log": a_log,
        "ssd_d": gain((Ld, SSD_HEADS)),
        "ssd_norm_w": gain((Ld, SSD_D_INNER)),
        "w_out": nrm((Ld, D_MIX, D_MODEL), D_MIX ** -0.5),
        "norm2_w": gain((Ld, D_MODEL)),
        "w_up": nrm((Ld, D_MODEL, D_FF), D_MODEL ** -0.5),
        "w_down": nrm((Ld, D_FF, D_MODEL), 0.5 * D_FF ** -0.5),
        "final_norm_w": gain((D_MODEL,)),
    }


def reference(x, norm1_w, w_in, gla_gate_w2, gla_gate_b, gla_norm_w,
              nsa_cmp_pos_k, nsa_cmp_w1_k, nsa_cmp_w2_k,
              nsa_cmp_pos_v, nsa_cmp_w1_v, nsa_cmp_w2_v, nsa_norm_w,
              ssd_conv_w, ssd_conv_b, ssd_dt_bias, ssd_a_log, ssd_d, ssd_norm_w,
              w_out, norm2_w, w_up, w_down, final_norm_w):
    S = x.shape[1]
    pos = jnp.arange(S)
    split_at = np.cumsum(IN_SPLITS)[:-1].tolist()
    h = x
    for l in range(DEPTH):
        u = rms_norm(h, norm1_w[l])
        proj = u @ w_in[l]
        (gq, gk, gv, glr, gr, nq_, nkv, ngate, sz, sxbc, sdt) = jnp.split(proj, split_at, axis=-1)
        y_gla = gla_mixer(gq, gk, gv, glr, gr, gla_gate_w2[l], gla_gate_b[l], gla_norm_w[l])
        y_nsa = nsa_mixer(nq_, nkv, ngate, pos,
                          nsa_cmp_pos_k[l], nsa_cmp_w1_k[l], nsa_cmp_w2_k[l],
                          nsa_cmp_pos_v[l], nsa_cmp_w1_v[l], nsa_cmp_w2_v[l], nsa_norm_w[l])
        y_ssd = ssd_mixer(sz, sxbc, sdt, ssd_conv_w[l], ssd_conv_b[l], ssd_dt_bias[l],
                          ssd_a_log[l], ssd_d[l], ssd_norm_w[l])
        mix = jnp.concatenate([y_gla, y_nsa, y_ssd], axis=-1)
        h = h + mix @ w_out[l]
        u = rms_norm(h, norm2_w[l])
        h = h + jnp.square(jax.nn.relu(u @ w_up[l])) @ w_down[l]
    return rms_norm(h, final_norm_w)
```

```python
import functools

import jax
import jax.numpy as jnp
import numpy as np
from jax import lax
from jax.experimental import pallas as pl
from jax.experimental.pallas import tpu as pltpu

F32 = jnp.float32
BF16 = jnp.bfloat16

D_MODEL = 1024
GLA_DK, GLA_DV = 32, 64
GLA_TAU = 16.0
GLA_SUB = 16
NSA_DH = 64
NSA_CMP_LEN, NSA_CMP_STRIDE = 32, 16
NSA_SEL_BLOCK, NSA_SEL_TOPK, NSA_SEL_LOCAL = 64, 16, 2
NSA_WINDOW = 512
SSD_D_INNER = 512
SSD_CONV = 4
SSD_CONV_DIM = 1024
D_FF = 4 * D_MODEL
ROPE_THETA = 10000.0
NORM_EPS = 1e-6
NEG = -1e30
BIG = 1e30
LANES = 128
VMEM_LIMIT = 56 * 1024 * 1024

IN_SPLITS = (128, 128, 256, 16, 256, 256, 768, 12, 512, SSD_CONV_DIM, 8)

TM_PROJ = 512
TT_GLA = 128
L_SSD = 128
TQ_NSA = 128
TK_NSA = 512
TM_MLP = 512
TF_MLP = 1024


def _cparams(sem):
    return pltpu.CompilerParams(dimension_semantics=sem, vmem_limit_bytes=VMEM_LIMIT)


def _dot(a, b):
    return jnp.dot(a, b, preferred_element_type=F32)


def _dot_nt(a, b):
    return lax.dot_general(a, b, (((1,), (1,)), ((), ())), preferred_element_type=F32)


def _split_dot_lhs(a, b_bf16, terms):
    acc, rem = None, a
    for _ in range(terms):
        hi = rem.astype(BF16)
        part = _dot(hi, b_bf16)
        acc = part if acc is None else acc + part
        rem = rem - hi.astype(F32)
    return acc


def _split_dot_rhs(a_bf16, b, terms):
    acc, rem = None, b
    for _ in range(terms):
        hi = rem.astype(BF16)
        part = _dot(a_bf16, hi)
        acc = part if acc is None else acc + part
        rem = rem - hi.astype(F32)
    return acc


def _sigmoid(x):
    return 1.0 / (1.0 + jnp.exp(-x))


def _softplus(x):
    return jnp.maximum(x, 0.0) + jnp.log1p(jnp.exp(-jnp.abs(x)))


def _rope_apply(x, cos, sin_signed, lane_lo):
    w = x.shape[-1]
    partner = jnp.where(lane_lo, pltpu.roll(x, w - 32, 1), pltpu.roll(x, 32, 1))
    return x * cos + partner * sin_signed


def _inproj_kernel(x_ref, nw_ref, cos_ref, sin_ref, wg_ref, wq_ref, wkv_ref, wgt_ref,
                   wz_ref, wx_ref, wdt_ref,
                   gla_ref, q_ref, cmp_ref, kv_ref, gate_ref, z_ref, xbc_ref, dt_ref):
    x = x_ref[...]
    u = x * lax.rsqrt(jnp.mean(x * x, axis=-1, keepdims=True) + NORM_EPS) * nw_ref[...]
    u = u.astype(BF16)
    gla_ref[...] = _dot(u, wg_ref[...])
    gate_ref[...] = _dot(u, wgt_ref[...])
    z_ref[...] = _dot(u, wz_ref[...])
    xbc_ref[...] = _dot(u, wx_ref[...])
    dt_ref[...] = _dot(u, wdt_ref[...])

    cos = cos_ref[...]
    sin = sin_ref[...]
    lane = lax.broadcasted_iota(jnp.int32, cos.shape, 1)
    lane_lo = (lane & 63) < 32
    q = _dot(u, wq_ref[...])
    for c in range(2):
        qc = _rope_apply(q[:, c * 128:(c + 1) * 128], cos, sin, lane_lo)
        q_ref[:, c * 128:(c + 1) * 128] = (qc * (NSA_DH ** -0.5)).astype(BF16)
    kv = _dot(u, wkv_ref[...])
    cmp_ref[:, 0:128] = _rope_apply(kv[:, 0:128], cos, sin, lane_lo)
    cmp_ref[:, 128:256] = kv[:, 128:256]
    kv_ref[:, 0:128] = _rope_apply(kv[:, 256:384], cos, sin, lane_lo).astype(BF16)
    kv_ref[:, 128:256] = kv[:, 384:512].astype(BF16)
    kv_ref[:, 256:384] = _rope_apply(kv[:, 512:640], cos, sin, lane_lo).astype(BF16)
    kv_ref[:, 384:512] = kv[:, 640:768].astype(BF16)


_PROJ_OUT = (("gla", 896, F32), ("q", 256, BF16), ("cmp", 256, F32), ("kv", 512, BF16),
             ("gate", 128, F32), ("z", 512, F32), ("xbc", 1024, F32), ("dt", 128, F32))
_PROJ_W = ("gla", "q", "kvall", "gate", "z", "xbc", "dt")


def _inproj(h2d, nw, cos, sin, w, seq):
    m = h2d.shape[0]
    tm = TM_PROJ
    nt = seq // tm
    row = lambda i: (i, 0)
    fixed = lambda i: (0, 0)
    outs = pl.pallas_call(
        _inproj_kernel,
        grid=(m // tm,),
        in_specs=[pl.BlockSpec((tm, D_MODEL), row),
                  pl.BlockSpec((1, D_MODEL), fixed),
                  pl.BlockSpec((tm, LANES), lambda i: (i % nt, 0)),
                  pl.BlockSpec((tm, LANES), lambda i: (i % nt, 0))]
                 + [pl.BlockSpec(w[n].shape, fixed) for n in _PROJ_W],
        out_specs=[pl.BlockSpec((tm, wd), row) for _, wd, _ in _PROJ_OUT],
        out_shape=[jax.ShapeDtypeStruct((m, wd), dt) for _, wd, dt in _PROJ_OUT],
        compiler_params=_cparams(("parallel",)),
        name="inproj",
    )(h2d, nw, cos, sin, *[w[n] for n in _PROJ_W])
    return {n: o for (n, _, _), o in zip(_PROJ_OUT, outs)}


def _gla_kernel(x_ref, w2_ref, bg_ref, nw_ref, tri_ref, hm_ref, hm2_ref, hmean_ref,
                o_ref, st_ref):
    tt = x_ref.shape[1]
    nsub = tt // GLA_SUB

    @pl.when(pl.program_id(1) == 0)
    def _():
        st_ref[...] = jnp.zeros_like(st_ref)

    x = x_ref[0]
    q = x[:, 0:128] * (GLA_DK ** -0.5)
    k = x[:, 128:256]
    v = x[:, 256:512]
    r = x[:, 512:768]
    glr = x[:, 768:896].astype(BF16)
    pre = _dot(glr, w2_ref[...]) + bg_ref[...]
    log_a = -_softplus(-pre) * (1.0 / GLA_TAU)
    bc = _split_dot_rhs(tri_ref[...], log_a, 3)

    q3 = q.reshape(nsub, GLA_SUB, 128)
    k3 = k.reshape(nsub, GLA_SUB, 128)
    v3 = v.reshape(nsub, GLA_SUB, 256)
    bc3 = bc.reshape(nsub, GLA_SUB, 128)
    row = lax.broadcasted_iota(jnp.int32, (nsub, GLA_SUB, 128), 1)
    hm = hm_ref[...]

    o = jnp.zeros((tt, 256), F32)
    for j in range(GLA_SUB):
        diff = bc3 - bc3[:, j:j + 1, :]
        w = jnp.exp(jnp.where(row >= j, diff, NEG))
        t = (q3 * k3[:, j:j + 1, :] * w).reshape(tt, 128).astype(BF16)
        a = _dot(t, hm).reshape(nsub, GLA_SUB, 256)
        o = o + (a * v3[:, j:j + 1, :]).reshape(tt, 256)

    gtot3 = bc3[:, GLA_SUB - 1:GLA_SUB, :]
    kdec = (k3 * jnp.exp(gtot3 - bc3)).reshape(tt, 128).astype(BF16)
    qdec = (q * jnp.exp(bc)).astype(BF16)
    vt = v.T.astype(BF16)
    col = lax.broadcasted_iota(jnp.int32, (256, tt), 1)
    hm2 = hm2_ref[...]
    st = st_ref[...]
    o_inter = []
    for c in range(nsub):
        lo = c * GLA_SUB
        o_inter.append(_dot_nt(qdec[lo:lo + GLA_SUB], st.astype(BF16)))
        vsel = jnp.where(col >= lo, jnp.where(col < lo + GLA_SUB, vt, 0.0), 0.0).astype(BF16)
        upd = _dot(vsel, kdec) * hm2
        st = st * jnp.exp(bc[lo + GLA_SUB - 1:lo + GLA_SUB, :]) + upd
    st_ref[...] = st
    o = o + jnp.concatenate(o_inter, axis=0)

    ms = _split_dot_lhs(o * o, hmean_ref[...], 2)
    y = o * lax.rsqrt(ms + NORM_EPS) * nw_ref[...]
    o_ref[0] = (y * (r * _sigmoid(r))).astype(o_ref.dtype)


def _gla(gla_in, w2, bg, nw, consts):
    b, s, _ = gla_in.shape
    tt = TT_GLA
    fixed = lambda bi, i: (0, 0)
    return pl.pallas_call(
        _gla_kernel,
        grid=(b, s // tt),
        in_specs=[pl.BlockSpec((1, tt, 896), lambda bi, i: (bi, i, 0)),
                  pl.BlockSpec((128, 128), fixed),
                  pl.BlockSpec((1, 128), fixed),
                  pl.BlockSpec((1, 256), fixed),
                  pl.BlockSpec((tt, tt), fixed),
                  pl.BlockSpec((128, 256), fixed),
                  pl.BlockSpec((256, 128), fixed),
                  pl.BlockSpec((256, 256), fixed)],
        out_specs=pl.BlockSpec((1, tt, 256), lambda bi, i: (bi, i, 0)),
        out_shape=jax.ShapeDtypeStruct((b, s, 256), BF16),
        scratch_shapes=[pltpu.VMEM((256, 128), F32)],
        compiler_params=_cparams(("parallel", "arbitrary")),
        name="gla",
    )(gla_in, w2, bg, nw, consts["gla_tri"], consts["gla_hm"], consts["gla_hm2"],
      consts["gla_hmean"])


def _ssd_kernel(z_ref, x_ref, halo_ref, dt_ref, cw_ref, cb_ref, dtb_ref, alog_ref,
                dskip_ref, nw_ref, tri_ref, ex_ref, o_ref, xe_ref, h_ref):
    L = x_ref.shape[1]
    i = pl.program_id(1)

    @pl.when(i == 0)
    def _():
        h_ref[...] = jnp.zeros_like(h_ref)

    halo = halo_ref[0]
    xe_ref[0:8, :] = jnp.where(i > 0, halo, jnp.zeros_like(halo))
    xe_ref[8:8 + L, :] = x_ref[0]
    cw = cw_ref[...]
    acc = jnp.broadcast_to(cb_ref[...], (L, SSD_CONV_DIM))
    for w in range(SSD_CONV):
        acc = acc + xe_ref[pl.ds(8 - (SSD_CONV - 1) + w, L), :] * cw[w:w + 1, :]
    xbc = acc * _sigmoid(acc)
    xs = xbc[:, 0:512]
    bm = xbc[:, 512:768]
    cm = xbc[:, 768:1024]

    dt = _softplus(dt_ref[0] + dtb_ref[...])
    a = dt * (-jnp.exp(alog_ref[...]))
    acs = _split_dot_rhs(tri_ref[...], a, 3)
    ex = ex_ref[...]
    dt_e = _split_dot_lhs(dt, ex, 3)
    acs_e = _split_dot_lhs(acs, ex, 3)
    acs_last_e = acs_e[L - 1:L, :]
    xdt = xs * dt_e
    xdt_b = xdt.astype(BF16)
    xd = (xdt * jnp.exp(acs_last_e - acs_e)).astype(BF16)
    eacs_e = jnp.exp(acs_e)
    cdec_e = jnp.exp(acs_last_e)
    acs_t = acs.T
    ri = lax.broadcasted_iota(jnp.int32, (L, L), 0)
    ci = lax.broadcasted_iota(jnp.int32, (L, L), 1)
    causal = ri >= ci

    ys = []
    for g in range(2):
        bg = bm[:, g * 128:(g + 1) * 128]
        cg = cm[:, g * 128:(g + 1) * 128].astype(BF16)
        cb = _dot_nt(cg, bg.astype(BF16))
        hg = h_ref[:, g * 256:(g + 1) * 256]
        yoff = _dot(cg, hg.astype(BF16)) * eacs_e[:, g * 256:(g + 1) * 256]
        ydiag = []
        for hh in range(4):
            hd = g * 4 + hh
            seg = jnp.exp(jnp.where(causal, acs[:, hd:hd + 1] - acs_t[hd:hd + 1, :], NEG))
            ydiag.append(_dot((cb * seg).astype(BF16), xdt_b[:, hd * 64:(hd + 1) * 64]))
        ys.append(yoff + jnp.concatenate(ydiag, axis=1))
        st = _dot(bg.T.astype(BF16), xd[:, g * 256:(g + 1) * 256])
        h_ref[:, g * 256:(g + 1) * 256] = hg * cdec_e[:, g * 256:(g + 1) * 256] + st

    y = jnp.concatenate(ys, axis=1) + dskip_ref[...] * xs
    zz = z_ref[0]
    y = y * (zz * _sigmoid(zz))
    nw = nw_ref[...]
    for g in range(2):
        yg = y[:, g * 256:(g + 1) * 256]
        ms = jnp.mean(yg * yg, axis=-1, keepdims=True)
        o_ref[0, :, g * 256:(g + 1) * 256] = (
            yg * lax.rsqrt(ms + NORM_EPS) * nw[:, g * 256:(g + 1) * 256]).astype(o_ref.dtype)


def _ssd(z, xbc, dt, cw, cb, dtb, alog, dskip_e, nw, consts):
    b, s, _ = z.shape
    L = L_SSD
    fixed = lambda bi, i: (0, 0)
    tok = lambda bi, i: (bi, i, 0)
    return pl.pallas_call(
        _ssd_kernel,
        grid=(b, s // L),
        in_specs=[pl.BlockSpec((1, L, 512), tok),
                  pl.BlockSpec((1, L, 1024), tok),
                  pl.BlockSpec((1, 8, 1024), lambda bi, i: (bi, jnp.maximum(i * (L // 8) - 1, 0), 0)),
                  pl.BlockSpec((1, L, 128), tok),
                  pl.BlockSpec((SSD_CONV, 1024), fixed),
                  pl.BlockSpec((1, 1024), fixed),
                  pl.BlockSpec((1, 128), fixed),
                  pl.BlockSpec((1, 128), fixed),
                  pl.BlockSpec((1, 512), fixed),
                  pl.BlockSpec((1, 512), fixed),
                  pl.BlockSpec((L, L), fixed),
                  pl.BlockSpec((128, 512), fixed)],
        out_specs=pl.BlockSpec((1, L, 512), tok),
        out_shape=jax.ShapeDtypeStruct((b, s, 512), BF16),
        scratch_shapes=[pltpu.VMEM((L + 8, 1024), F32), pltpu.VMEM((128, 512), F32)],
        compiler_params=_cparams(("parallel", "arbitrary")),
        name="ssd",
    )(z, xbc, xbc, dt, cw, cb, dtb, alog, dskip_e, nw, consts["ssd_tri"], consts["ssd_ex"])


def _cmp_kernel(sub_ref, pos_ref, w1_ref, w2_ref, o_ref):
    sub = sub_ref[0, 0, 0]
    nsb = sub.shape[0]
    pos = pos_ref[0]
    half = NSA_CMP_STRIDE * NSA_DH
    a = _dot((sub + pos[0:1, :]).astype(BF16), w1_ref[0, 0:half, :])
    b = _dot((sub + pos[1:2, :]).astype(BF16), w1_ref[0, half:2 * half, :])
    hid = a + pltpu.roll(b, nsb - 1, 0)
    gl = 0.5 * hid * (1.0 + jnp.tanh(0.7978845608028654 * (hid + 0.044715 * hid * hid * hid)))
    o_ref[0, 0, 0] = _dot(gl.astype(BF16), w2_ref[0])


def _compress(sub, pos, w1, w2):
    b, _, g, nsb, wd = sub.shape
    return pl.pallas_call(
        _cmp_kernel,
        grid=(b, 2, g),
        in_specs=[pl.BlockSpec((1, 1, 1, nsb, wd), lambda bi, c, gi: (bi, c, gi, 0, 0)),
                  pl.BlockSpec((1, 2, wd), lambda bi, c, gi: (c, 0, 0)),
                  pl.BlockSpec((1, 2 * wd, 128), lambda bi, c, gi: (c, 0, 0)),
                  pl.BlockSpec((1, 128, NSA_DH), lambda bi, c, gi: (c, 0, 0))],
        out_specs=pl.BlockSpec((1, 1, 1, nsb, NSA_DH), lambda bi, c, gi: (bi, c, gi, 0, 0)),
        out_shape=jax.ShapeDtypeStruct((b, 2, g, nsb, NSA_DH), F32),
        compiler_params=_cparams(("parallel", "parallel", "parallel")),
        name="nsa_compress",
    )(sub, pos, w1, w2)


def _softmax_rows(s, valid):
    m = jnp.max(s, axis=-1, keepdims=True)
    e = jnp.where(valid, jnp.exp(s - m), 0.0)
    return e, jnp.sum(e, axis=-1, keepdims=True)


def _nsa_kernel(q_ref, kc_ref, vc_ref, kv_ref, gate_ref, nw_ref, ov_ref, o_ref):
    tq = q_ref.shape[1]
    nsb = kc_ref.shape[1]
    tk = TK_NSA
    g = pl.program_id(1)
    t0 = pl.program_id(2) * tq
    isg0 = g == 0

    lane = lax.broadcasted_iota(jnp.int32, (tq, LANES), 1)
    gm = (lane >> 6) == g
    tpos = t0 + lax.broadcasted_iota(jnp.int32, (tq, 1), 0)

    qa = q_ref[0].astype(F32)
    qb = pltpu.roll(qa, 64, 1)
    q0 = jnp.where(gm, jnp.where(isg0, qa, qb), 0.0)
    q1 = jnp.where(gm, jnp.where(isg0, qb, qa), 0.0)
    q2 = jnp.concatenate([q0, q1], axis=0).astype(BF16)

    s = _dot_nt(q2, kc_ref[0]).reshape(2, tq, nsb)
    nidx = lax.broadcasted_iota(jnp.int32, (tq, nsb), 1)
    cvalid = (nidx * NSA_CMP_STRIDE + (NSA_CMP_LEN - 1)) <= tpos
    s = jnp.where(cvalid[None], s, NEG)
    e, den = _softmax_rows(s, cvalid[None])
    p = e / jnp.where(den > 0.0, den, 1.0)
    o_cmp = _dot(p.reshape(2 * tq, nsb).astype(BF16), vc_ref[0]).reshape(2, tq, LANES)
    imp = _split_dot_lhs(p[0] + p[1], ov_ref[...], 2)

    qblk = tpos >> 6
    forced = (lane == 0) | ((lane <= qblk) & (lane > qblk - NSA_SEL_LOCAL))
    imp = jnp.where(forced, BIG, jnp.where(lane > qblk, NEG, imp))
    lane_f = lane.astype(F32)

    def pick(_, carry):
        work, sel = carry
        mx = jnp.max(work, axis=-1, keepdims=True)
        first = jnp.min(jnp.where(work == mx, lane_f, 1e9), axis=-1, keepdims=True)
        hit = lane_f == first
        return jnp.where(hit, -3e38, work), jnp.where(hit, 1.0, sel)

    _, sel = lax.fori_loop(0, NSA_SEL_TOPK, pick, (imp, jnp.zeros((tq, LANES), F32)))
    sel = jnp.where(imp > 0.5 * NEG, sel, 0.0).astype(BF16)

    jrow = lax.broadcasted_iota(jnp.int32, (LANES, tk), 0)
    kcol = lax.broadcasted_iota(jnp.int32, (LANES, tk), 1)
    kcol_q = lax.broadcasted_iota(jnp.int32, (tq, tk), 1)

    def slc_step(c, carry):
        m, l, acc = carry
        k0 = pl.multiple_of(c * tk, tk)
        ks = kv_ref[0, pl.ds(k0, tk), 0:128]
        vs = kv_ref[0, pl.ds(k0, tk), 128:256]
        sc = _dot_nt(q2, ks).reshape(2, tq, tk)
        expand = jnp.where(jrow == ((k0 + kcol) >> 6), 1.0, 0.0).astype(BF16)
        chosen = _dot(sel, expand)
        valid = jnp.where((k0 + kcol_q) <= tpos, chosen, 0.0) > 0.5
        sc = jnp.where(valid[None], sc, NEG)
        m_new = jnp.maximum(m, jnp.max(sc, axis=-1, keepdims=True))
        alpha = jnp.exp(m - m_new)
        pe = jnp.where(valid[None], jnp.exp(sc - m_new), 0.0)
        l = alpha * l + jnp.sum(pe, axis=-1, keepdims=True)
        pv = _dot(pe.reshape(2 * tq, tk).astype(BF16), vs).reshape(2, tq, LANES)
        return m_new, l, alpha * acc + pv

    nkt = (t0 + tq + tk - 1) // tk
    init = (jnp.full((2, tq, 1), NEG, F32), jnp.zeros((2, tq, 1), F32),
            jnp.zeros((2, tq, LANES), F32))
    _, l_s, acc_s = lax.fori_loop(0, nkt, slc_step, init)
    o_slc = acc_s / l_s

    lw = tq + NSA_WINDOW
    w0 = pl.multiple_of(jnp.maximum(t0 - NSA_WINDOW, 0), tq)
    kw = kv_ref[0, pl.ds(w0, lw), 256:384]
    vw = kv_ref[0, pl.ds(w0, lw), 384:512]
    sw = _dot_nt(q2, kw).reshape(2, tq, lw)
    dist = tpos - (w0 + lax.broadcasted_iota(jnp.int32, (tq, lw), 1))
    wvalid = jnp.where(dist >= 0, dist, NSA_WINDOW) < NSA_WINDOW
    sw = jnp.where(wvalid[None], sw, NEG)
    ew, denw = _softmax_rows(sw, wvalid[None])
    o_win = _dot(ew.reshape(2 * tq, lw).astype(BF16), vw).reshape(2, tq, LANES) / denw

    sg = _sigmoid(gate_ref[0])
    outs = []
    for r in range(2):
        gcol = [jnp.sum(jnp.where(lane == g * 6 + r * 3 + c, sg, 0.0), axis=-1, keepdims=True)
                for c in range(3)]
        o = gcol[0] * o_cmp[r] + gcol[1] * o_slc[r] + gcol[2] * o_win[r]
        ms = jnp.sum(jnp.where(gm, o * o, 0.0), axis=-1, keepdims=True) * (1.0 / NSA_DH)
        outs.append(o * lax.rsqrt(ms + NORM_EPS))
    lo = jnp.where(isg0, outs[0], pltpu.roll(outs[0], 64, 1))
    hi = jnp.where(isg0, pltpu.roll(outs[1], 64, 1), outs[1])
    o_ref[0] = (jnp.where(lane < 64, lo, hi) * nw_ref[...]).astype(o_ref.dtype)


def _nsa(q, kc, vc, kv, gate, nw, consts):
    b, s, _ = q.shape
    nsb = kc.shape[1]
    tq = TQ_NSA
    return pl.pallas_call(
        _nsa_kernel,
        grid=(b, 2, s // tq),
        in_specs=[pl.BlockSpec((1, tq, 128), lambda bi, g, i: (bi, i, g)),
                  pl.BlockSpec((1, nsb, 128), lambda bi, g, i: (bi, 0, 0)),
                  pl.BlockSpec((1, nsb, 128), lambda bi, g, i: (bi, 0, 0)),
                  pl.BlockSpec((1, s, 512), lambda bi, g, i: (bi, 0, 0)),
                  pl.BlockSpec((1, tq, 128), lambda bi, g, i: (bi, i, 0)),
                  pl.BlockSpec((1, 128), lambda bi, g, i: (0, g)),
                  pl.BlockSpec((nsb, 128), lambda bi, g, i: (0, 0))],
        out_specs=pl.BlockSpec((1, tq, 128), lambda bi, g, i: (bi, i, g)),
        out_shape=jax.ShapeDtypeStruct((b, s, 256), BF16),
        compiler_params=_cparams(("parallel", "parallel", "arbitrary")),
        name="nsa_attn",
    )(q, kc, vc, kv, gate, nw, consts["nsa_overlap"])


def _outmlp_kernel(h_ref, yg_ref, yn_ref, ys_ref, wog_ref, won_ref, wos_ref, n2_ref,
                   wup_ref, wdn_ref, fn_ref, o_ref, acc_ref, u_ref, *, final_norm):
    k = pl.program_id(1)

    @pl.when(k == 0)
    def _():
        h2 = (h_ref[...] + _dot(yg_ref[...], wog_ref[...]) + _dot(yn_ref[...], won_ref[...])
              + _dot(ys_ref[...], wos_ref[...]))
        acc_ref[...] = h2
        u = h2 * lax.rsqrt(jnp.mean(h2 * h2, axis=-1, keepdims=True) + NORM_EPS) * n2_ref[...]
        u_ref[...] = u.astype(BF16)

    a = jnp.maximum(_dot(u_ref[...], wup_ref[...]), 0.0)
    acc_ref[...] += _dot((a * a).astype(BF16), wdn_ref[...])

    @pl.when(k == pl.num_programs(1) - 1)
    def _():
        out = acc_ref[...]
        if final_norm:
            out = (out * lax.rsqrt(jnp.mean(out * out, axis=-1, keepdims=True) + NORM_EPS)
                   * fn_ref[...])
        o_ref[...] = out


def _outmlp(h2d, yg, yn, ys, wog, won, wos, n2, wup, wdn, fn, final_norm):
    m = h2d.shape[0]
    tm, tf = TM_MLP, TF_MLP
    row = lambda i, k: (i, 0)
    fixed = lambda i, k: (0, 0)
    return pl.pallas_call(
        functools.partial(_outmlp_kernel, final_norm=final_norm),
        grid=(m // tm, D_FF // tf),
        in_specs=[pl.BlockSpec((tm, D_MODEL), row),
                  pl.BlockSpec((tm, 256), row),
                  pl.BlockSpec((tm, 256), row),
                  pl.BlockSpec((tm, 512), row),
                  pl.BlockSpec((256, D_MODEL), fixed),
                  pl.BlockSpec((256, D_MODEL), fixed),
                  pl.BlockSpec((512, D_MODEL), fixed),
                  pl.BlockSpec((1, D_MODEL), fixed),
                  pl.BlockSpec((D_MODEL, tf), lambda i, k: (0, k)),
                  pl.BlockSpec((tf, D_MODEL), lambda i, k: (k, 0)),
                  pl.BlockSpec((1, D_MODEL), fixed)],
        out_specs=pl.BlockSpec((tm, D_MODEL), row),
        out_shape=jax.ShapeDtypeStruct((m, D_MODEL), F32),
        scratch_shapes=[pltpu.VMEM((tm, D_MODEL), F32), pltpu.VMEM((tm, D_MODEL), BF16)],
        compiler_params=_cparams(("parallel", "arbitrary")),
        name="outproj_mlp",
    )(h2d, yg, yn, ys, wog, won, wos, n2, wup, wdn, fn)


def _constants(seq):
    c = {}
    i = np.arange(TT_GLA)
    c["gla_tri"] = ((i[:, None] >= i[None, :]) & (i[:, None] // GLA_SUB == i[None, :] // GLA_SUB))
    dk = np.arange(128) // GLA_DK
    dv = np.arange(256) // GLA_DV
    c["gla_hm"] = dk[:, None] == dv[None, :]
    c["gla_hm2"] = dv[:, None] == dk[None, :]
    c["gla_hmean"] = (dv[:, None] == dv[None, :]) / float(GLA_DV)
    i = np.arange(L_SSD)
    c["ssd_tri"] = i[:, None] >= i[None, :]
    hd = np.arange(128)
    c["ssd_ex"] = hd[:, None] == (np.arange(512) // 64)[None, :]
    nsb = seq // NSA_CMP_STRIDE
    cs = np.arange(nsb) * NSA_CMP_STRIDE
    ss = np.arange(LANES) * NSA_SEL_BLOCK
    c["nsa_overlap"] = ((cs[:, None] < ss[None, :] + NSA_SEL_BLOCK)
                        & (cs[:, None] + NSA_CMP_LEN > ss[None, :]))
    out = {k: jnp.asarray(np.asarray(v, np.float32), BF16) for k, v in c.items()}
    out["gla_hm2"] = out["gla_hm2"].astype(F32)
    return out


def _rope_tables(seq):
    half = NSA_DH // 2
    inv = ROPE_THETA ** (-jnp.arange(half, dtype=F32) / half)
    ang = jnp.arange(seq).astype(F32)[:, None] * inv[None, :]
    cos, sin = jnp.cos(ang), jnp.sin(ang)
    return jnp.tile(cos, (1, 4)), jnp.tile(jnp.concatenate([-sin, sin], axis=1), (1, 2))


def _pad_cols(w, width):
    return jnp.pad(w, ((0, 0), (0, width - w.shape[1])))


def _proj_weights(w_in):
    gq, gk, gv, glr, gr, nq, nkv, ngate, sz, sxbc, sdt = jnp.split(
        w_in, np.cumsum(IN_SPLITS)[:-1].tolist(), axis=1)
    w = dict(gla=jnp.concatenate([gq, gk, gv, gr, _pad_cols(glr, 128)], axis=1),
             q=nq, kvall=nkv, gate=_pad_cols(ngate, 128), z=sz, xbc=sxbc, dt=_pad_cols(sdt, 128))
    return {k: v.astype(BF16) for k, v in w.items()}


def kernel(x, norm1_w, w_in, gla_gate_w2, gla_gate_b, gla_norm_w, nsa_cmp_pos_k, nsa_cmp_w1_k, nsa_cmp_w2_k, nsa_cmp_pos_v, nsa_cmp_w1_v, nsa_cmp_w2_v, nsa_norm_w, ssd_conv_w, ssd_conv_b, ssd_dt_bias, ssd_a_log, ssd_d, ssd_norm_w, w_out, norm2_w, w_up, w_down, final_norm_w):
    bsz, seq, _ = x.shape
    depth = w_in.shape[0]
    m = bsz * seq
    nsb = seq // NSA_CMP_STRIDE
    assert seq % (4 * TK_NSA) == 0 and seq // NSA_SEL_BLOCK <= LANES
    consts = _constants(seq)
    cos, sin = _rope_tables(seq)
    h = x.reshape(m, D_MODEL)
    for l in range(depth):
        p = _inproj(h, norm1_w[l][None, :], cos, sin, _proj_weights(w_in[l]), seq)

        y_gla = _gla(p["gla"].reshape(bsz, seq, 896),
                     jnp.pad(gla_gate_w2[l], ((0, 112), (0, 0))).astype(BF16),
                     gla_gate_b[l][None, :], gla_norm_w[l][None, :], consts)

        y_ssd = _ssd(p["z"].reshape(bsz, seq, 512), p["xbc"].reshape(bsz, seq, 1024),
                     p["dt"].reshape(bsz, seq, 128), ssd_conv_w[l], ssd_conv_b[l][None, :],
                     jnp.pad(ssd_dt_bias[l], (0, 120))[None, :],
                     jnp.pad(ssd_a_log[l], (0, 120))[None, :],
                     jnp.repeat(ssd_d[l], 64)[None, :], ssd_norm_w[l][None, :], consts)

        sub = p["cmp"].reshape(bsz, nsb, NSA_CMP_STRIDE, 2, 2, NSA_DH)
        sub = sub.transpose(0, 3, 4, 1, 2, 5).reshape(bsz, 2, 2, nsb, NSA_CMP_STRIDE * NSA_DH)
        pos = jnp.stack([nsa_cmp_pos_k[l], nsa_cmp_pos_v[l]]).reshape(2, 2, NSA_CMP_STRIDE * NSA_DH)
        w1 = jnp.stack([nsa_cmp_w1_k[l], nsa_cmp_w1_v[l]]).astype(BF16)
        w2 = jnp.stack([nsa_cmp_w2_k[l], nsa_cmp_w2_v[l]]).astype(BF16)
        cmp = _compress(sub, pos, w1, w2)
        cmp = cmp.transpose(1, 0, 3, 2, 4).reshape(2, bsz, nsb, 128).astype(BF16)

        y_nsa = _nsa(p["q"].reshape(bsz, seq, 256), cmp[0], cmp[1],
                     p["kv"].reshape(bsz, seq, 512), p["gate"].reshape(bsz, seq, 128),
                     nsa_norm_w[l][None, :], consts)

        wo = w_out[l].astype(BF16)
        h = _outmlp(h, y_gla.reshape(m, 256), y_nsa.reshape(m, 256), y_ssd.reshape(m, 512),
                    wo[0:256], wo[256:512], wo[512:1024], norm2_w[l][None, :],
                    w_up[l].astype(BF16), w_down[l].astype(BF16), final_norm_w[None, :],
                    final_norm=(l == depth - 1))
    return h.reshape(bsz, seq, D_MODEL)
```

```python
import functools

import jax
import jax.numpy as jnp
import numpy as np
from jax import lax
from jax.experimental import pallas as pl
from jax.experimental.pallas import tpu as pltpu

F32 = jnp.float32
BF16 = jnp.bfloat16

D_MODEL = 1024
GLA_DK, GLA_DV = 32, 64
GLA_TAU = 16.0
GLA_SUB = 16
NSA_DH = 64
NSA_CMP_LEN, NSA_CMP_STRIDE = 32, 16
NSA_SEL_BLOCK, NSA_SEL_TOPK, NSA_SEL_LOCAL = 64, 16, 2
NSA_WINDOW = 512
SSD_D_INNER = 512
SSD_CONV = 4
SSD_CONV_DIM = 1024
D_FF = 4 * D_MODEL
ROPE_THETA = 10000.0
NORM_EPS = 1e-6
NEG = -1e30
BIG = 1e30
LANES = 128
VMEM_LIMIT = 56 * 1024 * 1024

IN_SPLITS = (128, 128, 256, 16, 256, 256, 768, 12, 512, SSD_CONV_DIM, 8)

TM_PROJ = 512
TT_GLA = 128
L_SSD = 128
TQ_NSA = 256
TK_NSA = 1024
NSA_FLAGS = TK_NSA // NSA_SEL_BLOCK
TM_MLP = 512
TF_MLP = 1024


def _cparams(sem):
    return pltpu.CompilerParams(dimension_semantics=sem, vmem_limit_bytes=VMEM_LIMIT)


def _dot(a, b):
    return jnp.dot(a, b, preferred_element_type=F32)


def _dot_nt(a, b):
    return lax.dot_general(a, b, (((1,), (1,)), ((), ())), preferred_element_type=F32)


def _split_dot_lhs(a, b_bf16, terms):
    acc, rem = None, a
    for _ in range(terms):
        hi = rem.astype(BF16)
        part = _dot(hi, b_bf16)
        acc = part if acc is None else acc + part
        rem = rem - hi.astype(F32)
    return acc


def _split_dot_rhs(a_bf16, b, terms):
    acc, rem = None, b
    for _ in range(terms):
        hi = rem.astype(BF16)
        part = _dot(a_bf16, hi)
        acc = part if acc is None else acc + part
        rem = rem - hi.astype(F32)
    return acc


def _sigmoid(x):
    return 1.0 / (1.0 + jnp.exp(-x))


def _softplus(x):
    return jnp.maximum(x, 0.0) + jnp.log1p(jnp.exp(-jnp.abs(x)))


def _rope_apply(x, cos, sin_signed, lane_lo):
    w = x.shape[-1]
    partner = jnp.where(lane_lo, pltpu.roll(x, w - 32, 1), pltpu.roll(x, 32, 1))
    return x * cos + partner * sin_signed


def _inproj_kernel(x_ref, nw_ref, cos_ref, sin_ref, wg_ref, wq_ref, wkv_ref, wgt_ref,
                   wz_ref, wx_ref, wdt_ref,
                   gla_ref, q_ref, cmp_ref, ks_ref, kv_ref, gate_ref, z_ref, xbc_ref, dt_ref,
                   *, tiles_per_seq):
    x = x_ref[...]
    u = x * lax.rsqrt(jnp.mean(x * x, axis=-1, keepdims=True) + NORM_EPS) * nw_ref[...]
    u = u.astype(BF16)
    gla_ref[...] = _dot(u, wg_ref[...])
    gate_ref[...] = _dot(u, wgt_ref[...])
    z_ref[...] = _dot(u, wz_ref[...])
    xbc_ref[...] = _dot(u, wx_ref[...])
    dt_ref[...] = _dot(u, wdt_ref[...])

    cos = cos_ref[...]
    sin = sin_ref[...]
    lane = lax.broadcasted_iota(jnp.int32, cos.shape, 1)
    lane_lo = (lane & 63) < 32
    q = _dot(u, wq_ref[...])
    for c in range(2):
        qc = _rope_apply(q[:, c * 128:(c + 1) * 128], cos, sin, lane_lo)
        q_ref[:, c * 128:(c + 1) * 128] = (qc * (NSA_DH ** -0.5)).astype(BF16)
    kv = _dot(u, wkv_ref[...])
    cmp_ref[:, 0:128] = _rope_apply(kv[:, 0:128], cos, sin, lane_lo)
    cmp_ref[:, 128:256] = kv[:, 128:256]
    ksl = _rope_apply(kv[:, 256:384], cos, sin, lane_lo)
    tm = x.shape[0]
    pos = ((pl.program_id(0) % tiles_per_seq) * tm
           + lax.broadcasted_iota(jnp.int32, (tm, 1), 0))
    flag = jnp.where((lane - 64) == ((pos >> 6) & (NSA_FLAGS - 1)), BIG, 0.0)
    ks_ref[0] = jnp.where(lane < 64, ksl, flag).astype(BF16)
    ks_ref[1] = jnp.where(lane < 64, pltpu.roll(ksl, 64, 1), flag).astype(BF16)
    kv_ref[:, 0:128] = kv[:, 384:512].astype(BF16)
    kv_ref[:, 128:256] = _rope_apply(kv[:, 512:640], cos, sin, lane_lo).astype(BF16)
    kv_ref[:, 256:384] = kv[:, 640:768].astype(BF16)


_PROJ_OUT = (("gla", 896, F32), ("q", 256, BF16), ("cmp", 256, F32), ("ks", 128, BF16),
             ("kv", 384, BF16), ("gate", 128, F32), ("z", 512, F32), ("xbc", 1024, F32),
             ("dt", 128, F32))
_PROJ_W = ("gla", "q", "kvall", "gate", "z", "xbc", "dt")


def _inproj(h2d, nw, cos, sin, w, layer, seq):
    m = h2d.shape[0]
    tm = TM_PROJ
    nt = seq // tm
    row = lambda i: (i, 0)
    fixed = lambda i: (0, 0)

    def ospec(name, wd):
        if name == "ks":
            return pl.BlockSpec((2, tm, wd), lambda i: (0, i, 0))
        return pl.BlockSpec((tm, wd), row)

    def oshape(name, wd, dt):
        return jax.ShapeDtypeStruct((2, m, wd) if name == "ks" else (m, wd), dt)

    outs = pl.pallas_call(
        functools.partial(_inproj_kernel, tiles_per_seq=nt),
        grid=(m // tm,),
        in_specs=[pl.BlockSpec((tm, D_MODEL), row),
                  pl.BlockSpec((1, D_MODEL), fixed),
                  pl.BlockSpec((tm, LANES), lambda i: (i % nt, 0)),
                  pl.BlockSpec((tm, LANES), lambda i: (i % nt, 0))]
                 + [pl.BlockSpec((None,) + w[n].shape[1:], lambda i: (layer, 0, 0))
                    for n in _PROJ_W],
        out_specs=[ospec(n, wd) for n, wd, _ in _PROJ_OUT],
        out_shape=[oshape(n, wd, dt) for n, wd, dt in _PROJ_OUT],
        compiler_params=_cparams(("parallel",)),
        name="inproj",
    )(h2d, nw, cos, sin, *[w[n] for n in _PROJ_W])
    return {n: o for (n, _, _), o in zip(_PROJ_OUT, outs)}


def _gla_kernel(x_ref, w2_ref, bg_ref, nw_ref, tri_ref, hm_ref, hm2_ref, hmean_ref,
                o_ref, st_ref):
    tt = x_ref.shape[1]
    nsub = tt // GLA_SUB

    @pl.when(pl.program_id(1) == 0)
    def _():
        st_ref[...] = jnp.zeros_like(st_ref)

    x = x_ref[0]
    q = x[:, 0:128] * (GLA_DK ** -0.5)
    k = x[:, 128:256]
    v = x[:, 256:512]
    r = x[:, 512:768]
    glr = x[:, 768:896].astype(BF16)
    pre = _dot(glr, w2_ref[...]) + bg_ref[...]
    log_a = -_softplus(-pre) * (1.0 / GLA_TAU)
    bc = _split_dot_rhs(tri_ref[...], log_a, 3)

    q3 = q.reshape(nsub, GLA_SUB, 128)
    k3 = k.reshape(nsub, GLA_SUB, 128)
    v3 = v.reshape(nsub, GLA_SUB, 256)
    bc3 = bc.reshape(nsub, GLA_SUB, 128)
    row = lax.broadcasted_iota(jnp.int32, (nsub, GLA_SUB, 128), 1)
    hm = hm_ref[...]

    o = jnp.zeros((tt, 256), F32)
    for j in range(GLA_SUB):
        diff = bc3 - bc3[:, j:j + 1, :]
        w = jnp.exp(jnp.where(row >= j, diff, NEG))
        t = (q3 * k3[:, j:j + 1, :] * w).reshape(tt, 128).astype(BF16)
        a = _dot(t, hm).reshape(nsub, GLA_SUB, 256)
        o = o + (a * v3[:, j:j + 1, :]).reshape(tt, 256)

    gtot3 = bc3[:, GLA_SUB - 1:GLA_SUB, :]
    kdec = (k3 * jnp.exp(gtot3 - bc3)).reshape(tt, 128).astype(BF16)
    qdec = q * jnp.exp(bc)
    bends = [bc[GLA_SUB - 1:GLA_SUB, :]]
    for c in range(1, nsub):
        bends.append(bends[-1] + bc[(c + 1) * GLA_SUB - 1:(c + 1) * GLA_SUB, :])
    bstart = jnp.concatenate(
        [jnp.zeros((GLA_SUB, 128), F32)]
        + [jnp.broadcast_to(bends[c], (GLA_SUB, 128)) for c in range(nsub - 1)], axis=0)
    st0 = st_ref[...]
    o = o + _dot_nt((qdec * jnp.exp(bstart)).astype(BF16), st0.astype(BF16))

    vt = v.T.astype(BF16)
    col = lax.broadcasted_iota(jnp.int32, (256, tt), 1)
    rowi = lax.broadcasted_iota(jnp.int32, (tt, 128), 0)
    hm2 = hm2_ref[...]
    st = st0 * jnp.exp(bends[nsub - 1])
    for b in range(nsub):
        lo = b * GLA_SUB
        vsel = jnp.where(col >= lo, jnp.where(col < lo + GLA_SUB, vt, 0.0), 0.0).astype(BF16)
        upd = _dot(vsel, kdec) * hm2
        if b < nsub - 1:
            later = jnp.exp(jnp.where(rowi >= lo + GLA_SUB, bstart - bends[b], NEG))
            o = o + _dot_nt((qdec * later).astype(BF16), upd.astype(BF16))
            st = st + upd * jnp.exp(bends[nsub - 1] - bends[b])
        else:
            st = st + upd
    st_ref[...] = st

    ms = _split_dot_lhs(o * o, hmean_ref[...], 2)
    y = o * lax.rsqrt(ms + NORM_EPS) * nw_ref[...]
    o_ref[0] = (y * (r * _sigmoid(r))).astype(o_ref.dtype)


def _gla(gla_in, w2, bg, nw, consts):
    b, s, _ = gla_in.shape
    tt = TT_GLA
    fixed = lambda bi, i: (0, 0)
    return pl.pallas_call(
        _gla_kernel,
        grid=(b, s // tt),
        in_specs=[pl.BlockSpec((1, tt, 896), lambda bi, i: (bi, i, 0)),
                  pl.BlockSpec((128, 128), fixed),
                  pl.BlockSpec((1, 128), fixed),
                  pl.BlockSpec((1, 256), fixed),
                  pl.BlockSpec((tt, tt), fixed),
                  pl.BlockSpec((128, 256), fixed),
                  pl.BlockSpec((256, 128), fixed),
                  pl.BlockSpec((256, 256), fixed)],
        out_specs=pl.BlockSpec((1, tt, 256), lambda bi, i: (bi, i, 0)),
        out_shape=jax.ShapeDtypeStruct((b, s, 256), BF16),
        scratch_shapes=[pltpu.VMEM((256, 128), F32)],
        compiler_params=_cparams(("parallel", "arbitrary")),
        name="gla",
    )(gla_in, w2, bg, nw, consts["gla_tri"], consts["gla_hm"], consts["gla_hm2"],
      consts["gla_hmean"])


def _ssd_kernel(z_ref, x_ref, halo_ref, dt_ref, cw_ref, cb_ref, dtb_ref, alog_ref,
                dskip_ref, nw_ref, tri_ref, ex_ref, o_ref, xe_ref, h_ref):
    L = x_ref.shape[1]
    i = pl.program_id(1)

    @pl.when(i == 0)
    def _():
        h_ref[...] = jnp.zeros_like(h_ref)

    halo = halo_ref[0]
    xe_ref[0:8, :] = jnp.where(i > 0, halo, jnp.zeros_like(halo))
    xe_ref[8:8 + L, :] = x_ref[0]
    cw = cw_ref[...]
    acc = jnp.broadcast_to(cb_ref[...], (L, SSD_CONV_DIM))
    for w in range(SSD_CONV):
        acc = acc + xe_ref[pl.ds(8 - (SSD_CONV - 1) + w, L), :] * cw[w:w + 1, :]
    xbc = acc * _sigmoid(acc)
    xs = xbc[:, 0:512]
    bm = xbc[:, 512:768]
    cm = xbc[:, 768:1024]

    dt = _softplus(dt_ref[0] + dtb_ref[...])
    a = dt * (-jnp.exp(alog_ref[...]))
    acs = _split_dot_rhs(tri_ref[...], a, 3)
    ex = ex_ref[...]
    dt_e = _split_dot_lhs(dt, ex, 3)
    acs_e = _split_dot_lhs(acs, ex, 3)
    acs_last_e = acs_e[L - 1:L, :]
    xdt = xs * dt_e
    xdt_b = xdt.astype(BF16)
    xd = (xdt * jnp.exp(acs_last_e - acs_e)).astype(BF16)
    eacs_e = jnp.exp(acs_e)
    cdec_e = jnp.exp(acs_last_e)
    acs_t = acs.T
    ri = lax.broadcasted_iota(jnp.int32, (L, L), 0)
    ci = lax.broadcasted_iota(jnp.int32, (L, L), 1)
    causal = ri >= ci

    ys = []
    for g in range(2):
        bg = bm[:, g * 128:(g + 1) * 128]
        cg = cm[:, g * 128:(g + 1) * 128].astype(BF16)
        cb = _dot_nt(cg, bg.astype(BF16))
        hg = h_ref[:, g * 256:(g + 1) * 256]
        yoff = _dot(cg, hg.astype(BF16)) * eacs_e[:, g * 256:(g + 1) * 256]
        ydiag = []
        for hh in range(4):
            hd = g * 4 + hh
            seg = jnp.exp(jnp.where(causal, acs[:, hd:hd + 1] - acs_t[hd:hd + 1, :], NEG))
            ydiag.append(_dot((cb * seg).astype(BF16), xdt_b[:, hd * 64:(hd + 1) * 64]))
        ys.append(yoff + jnp.concatenate(ydiag, axis=1))
        st = _dot(bg.T.astype(BF16), xd[:, g * 256:(g + 1) * 256])
        h_ref[:, g * 256:(g + 1) * 256] = hg * cdec_e[:, g * 256:(g + 1) * 256] + st

    y = jnp.concatenate(ys, axis=1) + dskip_ref[...] * xs
    zz = z_ref[0]
    y = y * (zz * _sigmoid(zz))
    nw = nw_ref[...]
    for g in range(2):
        yg = y[:, g * 256:(g + 1) * 256]
        ms = jnp.mean(yg * yg, axis=-1, keepdims=True)
        o_ref[0, :, g * 256:(g + 1) * 256] = (
            yg * lax.rsqrt(ms + NORM_EPS) * nw[:, g * 256:(g + 1) * 256]).astype(o_ref.dtype)


def _ssd(z, xbc, dt, cw, cb, dtb, alog, dskip_e, nw, consts):
    b, s, _ = z.shape
    L = L_SSD
    fixed = lambda bi, i: (0, 0)
    tok = lambda bi, i: (bi, i, 0)
    return pl.pallas_call(
        _ssd_kernel,
        grid=(b, s // L),
        in_specs=[pl.BlockSpec((1, L, 512), tok),
                  pl.BlockSpec((1, L, 1024), tok),
                  pl.BlockSpec((1, 8, 1024), lambda bi, i: (bi, jnp.maximum(i * (L // 8) - 1, 0), 0)),
                  pl.BlockSpec((1, L, 128), tok),
                  pl.BlockSpec((SSD_CONV, 1024), fixed),
                  pl.BlockSpec((1, 1024), fixed),
                  pl.BlockSpec((1, 128), fixed),
                  pl.BlockSpec((1, 128), fixed),
                  pl.BlockSpec((1, 512), fixed),
                  pl.BlockSpec((1, 512), fixed),
                  pl.BlockSpec((L, L), fixed),
                  pl.BlockSpec((128, 512), fixed)],
        out_specs=pl.BlockSpec((1, L, 512), tok),
        out_shape=jax.ShapeDtypeStruct((b, s, 512), BF16),
        scratch_shapes=[pltpu.VMEM((L + 8, 1024), F32), pltpu.VMEM((128, 512), F32)],
        compiler_params=_cparams(("parallel", "arbitrary")),
        name="ssd",
    )(z, xbc, xbc, dt, cw, cb, dtb, alog, dskip_e, nw, consts["ssd_tri"], consts["ssd_ex"])


def _cmp_kernel(x_ref, pos_ref, w1_ref, w2_ref, o_ref):
    nsb = o_ref.shape[2]
    pos = pos_ref[0]
    lo = jnp.zeros((nsb, 256), F32)
    hi = jnp.zeros((nsb, 256), F32)
    for t in range(NSA_CMP_STRIDE):
        xt = x_ref[0, pl.ds(t, nsb, stride=NSA_CMP_STRIDE), :]
        lo = lo + _dot((xt + pos[t:t + 1, :]).astype(BF16), w1_ref[0, t])
        u = t + NSA_CMP_STRIDE
        hi = hi + _dot((xt + pos[u:u + 1, :]).astype(BF16), w1_ref[0, u])
    hid = lo + pltpu.roll(hi, nsb - 1, 0)
    gl = 0.5 * hid * (1.0 + jnp.tanh(0.7978845608028654 * (hid + 0.044715 * hid * hid * hid)))
    o_ref[0, 0] = _dot(gl.astype(BF16), w2_ref[0]).astype(o_ref.dtype)


def _compress(x, pos, w1, w2):
    b, s, _ = x.shape
    nsb = s // NSA_CMP_STRIDE
    return pl.pallas_call(
        _cmp_kernel,
        grid=(b, 2),
        in_specs=[pl.BlockSpec((1, s, 128), lambda bi, c: (bi, 0, c)),
                  pl.BlockSpec((1, NSA_CMP_LEN, 128), lambda bi, c: (c, 0, 0)),
                  pl.BlockSpec((1, NSA_CMP_LEN, 128, 256), lambda bi, c: (c, 0, 0, 0)),
                  pl.BlockSpec((1, 256, 128), lambda bi, c: (c, 0, 0))],
        out_specs=pl.BlockSpec((1, 1, nsb, 128), lambda bi, c: (c, bi, 0, 0)),
        out_shape=jax.ShapeDtypeStruct((2, b, nsb, 128), BF16),
        compiler_params=_cparams(("parallel", "parallel")),
        name="nsa_compress",
    )(x, pos, w1, w2)


def _block_diag2(w):
    z = jnp.zeros_like(w)
    return jnp.concatenate([jnp.concatenate([w, z], axis=-1),
                            jnp.concatenate([z, w], axis=-1)], axis=-2)


def _softmax_rows(s, valid):
    m = jnp.max(s, axis=-1, keepdims=True)
    e = jnp.where(valid, jnp.exp(s - m), 0.0)
    return e, jnp.sum(e, axis=-1, keepdims=True)


def _nsa_kernel(q_ref, kc_ref, vc_ref, ks_ref, kv_ref, gate_ref, nw_ref, ov_ref, o_ref):
    tq = q_ref.shape[1]
    nsb = kc_ref.shape[1]
    tk = TK_NSA
    t0 = pl.program_id(1) * tq
    groups = (0, 1)

    lane = lax.broadcasted_iota(jnp.int32, (tq, LANES), 1)
    lane_lo = lane < 64
    gmask = (lane_lo, lane >= 64)
    tpos = t0 + lax.broadcasted_iota(jnp.int32, (tq, 1), 0)

    q2, qs = [], []
    for g in groups:
        qa = q_ref[0, :, g * 128:(g + 1) * 128].astype(F32)
        qb = pltpu.roll(qa, 64, 1)
        h0, h1 = (qa, qb) if g == 0 else (qb, qa)
        q2.append(jnp.concatenate([jnp.where(gmask[g], h0, 0.0), jnp.where(gmask[g], h1, 0.0)],
                                  axis=0).astype(BF16))
        qs.append((jnp.where(lane_lo, qa, 0.0), jnp.where(lane_lo, qb, 0.0)))

    nidx = lax.broadcasted_iota(jnp.int32, (tq, nsb), 1)
    cvalid = (nidx * NSA_CMP_STRIDE + (NSA_CMP_LEN - 1)) <= tpos
    o_cmp, imp_ts = [], []
    for g in groups:
        s = _dot_nt(q2[g], kc_ref[0]).reshape(2, tq, nsb)
        s = jnp.where(cvalid[None], s, NEG)
        e, den = _softmax_rows(s, cvalid[None])
        p = e / jnp.where(den > 0.0, den, 1.0)
        o_cmp.append(_dot(p.reshape(2 * tq, nsb).astype(BF16), vc_ref[0]).reshape(2, tq, LANES))
        imp_ts.append(_split_dot_lhs(p[0] + p[1], ov_ref[...], 2).T)

    imp_t = jnp.concatenate(imp_ts, axis=1)
    blk = lax.broadcasted_iota(jnp.int32, (LANES, 2 * tq), 0)
    tcol = lax.broadcasted_iota(jnp.int32, (1, 2 * tq), 1)
    qblk = (t0 + jnp.where(tcol >= tq, tcol - tq, tcol)) >> 6
    forced = (blk == 0) | ((blk <= qblk) & (blk > qblk - NSA_SEL_LOCAL))
    imp_t = jnp.where(forced, BIG, jnp.where(blk > qblk, NEG, imp_t))
    blk_f = blk.astype(F32)

    def pick(_, work):
        mx = jnp.max(work, axis=0, keepdims=True)
        first = jnp.min(jnp.where(work == mx, blk_f, 1e9), axis=0, keepdims=True)
        return jnp.where(blk_f == first, -3e38, work)

    work = lax.fori_loop(0, NSA_SEL_TOPK, pick, imp_t)
    veto_t = jnp.where(work < -2e38, jnp.where(imp_t > 0.5 * NEG, 0.0, -1.0), -1.0)
    veto = [veto_t[:, g * tq:(g + 1) * tq].T for g in groups]

    kcol_q = lax.broadcasted_iota(jnp.int32, (tq, tk), 1)
    flag_lane = (lane >= 64) & (lane < 64 + NSA_FLAGS)

    def slc_step(c, carry, diagonal):
        k0 = pl.multiple_of(c * tk, tk)
        vs = kv_ref[0, pl.ds(k0, tk), 0:128]
        shift = (64 - NSA_FLAGS * c) & 127
        out = []
        for g in groups:
            m, l, acc = carry[g]
            ks = ks_ref[g, 0, pl.ds(k0, tk), :]
            aug = jnp.where(flag_lane, pltpu.roll(veto[g], shift, 1), 0.0)
            qq = jnp.concatenate([qs[g][0] + aug, qs[g][1] + aug], axis=0).astype(BF16)
            sc = _dot_nt(qq, ks).reshape(2, tq, tk)
            if diagonal:
                sc = jnp.where(((k0 + kcol_q) <= tpos)[None], sc, NEG)
            m_new = jnp.maximum(m, jnp.max(sc, axis=-1, keepdims=True))
            alpha = jnp.exp(m - m_new)
            pe = jnp.exp(sc - m_new)
            l = alpha * l + jnp.sum(pe, axis=-1, keepdims=True)
            pv = _dot(pe.reshape(2 * tq, tk).astype(BF16), vs).reshape(2, tq, LANES)
            out.append((m_new, l, alpha * acc + pv))
        return tuple(out)

    nkt = (t0 + tq + tk - 1) // tk
    init = (jnp.full((2, tq, 1), NEG, F32), jnp.zeros((2, tq, 1), F32),
            jnp.zeros((2, tq, LANES), F32))
    carry = lax.fori_loop(0, nkt - 1, functools.partial(slc_step, diagonal=False), (init, init))
    carry = slc_step(nkt - 1, carry, diagonal=True)

    lw = tq + NSA_WINDOW
    w0 = pl.multiple_of(jnp.maximum(t0 - NSA_WINDOW, 0), tq)
    kw = kv_ref[0, pl.ds(w0, lw), 128:256]
    vw = kv_ref[0, pl.ds(w0, lw), 256:384]
    dist = tpos - (w0 + lax.broadcasted_iota(jnp.int32, (tq, lw), 1))
    wvalid = jnp.where(dist >= 0, dist, NSA_WINDOW) < NSA_WINDOW

    sg = _sigmoid(gate_ref[0])
    for g in groups:
        _, l_s, acc_s = carry[g]
        o_slc = acc_s / l_s
        sw = _dot_nt(q2[g], kw).reshape(2, tq, lw)
        sw = jnp.where(wvalid[None], sw, NEG)
        ew, denw = _softmax_rows(sw, wvalid[None])
        o_win = _dot(ew.reshape(2 * tq, lw).astype(BF16), vw).reshape(2, tq, LANES) / denw
        outs = []
        for r in range(2):
            gcol = [jnp.sum(jnp.where(lane == g * 6 + r * 3 + c, sg, 0.0), axis=-1, keepdims=True)
                    for c in range(3)]
            o = gcol[0] * o_cmp[g][r] + gcol[1] * o_slc[r] + gcol[2] * o_win[r]
            ms = jnp.sum(jnp.where(gmask[g], o * o, 0.0), axis=-1, keepdims=True) * (1.0 / NSA_DH)
            outs.append(o * lax.rsqrt(ms + NORM_EPS))
        lo = outs[0] if g == 0 else pltpu.roll(outs[0], 64, 1)
        hi = pltpu.roll(outs[1], 64, 1) if g == 0 else outs[1]
        o_ref[0, :, g * 128:(g + 1) * 128] = (
            jnp.where(lane_lo, lo, hi) * nw_ref[:, g * 128:(g + 1) * 128]).astype(o_ref.dtype)


def _nsa(q, cmp, ks, kv, gate, nw, consts):
    b, s, _ = q.shape
    nsb = cmp.shape[2]
    tq = TQ_NSA
    return pl.pallas_call(
        _nsa_kernel,
        grid=(b, s // tq),
        in_specs=[pl.BlockSpec((1, tq, 256), lambda bi, i: (bi, i, 0)),
                  pl.BlockSpec((None, 1, nsb, 128), lambda bi, i: (0, bi, 0, 0)),
                  pl.BlockSpec((None, 1, nsb, 128), lambda bi, i: (1, bi, 0, 0)),
                  pl.BlockSpec((2, 1, s, 128), lambda bi, i: (0, bi, 0, 0)),
                  pl.BlockSpec((1, s, 384), lambda bi, i: (bi, 0, 0)),
                  pl.BlockSpec((1, tq, 128), lambda bi, i: (bi, i, 0)),
                  pl.BlockSpec((1, 256), lambda bi, i: (0, 0)),
                  pl.BlockSpec((nsb, 128), lambda bi, i: (0, 0))],
        out_specs=pl.BlockSpec((1, tq, 256), lambda bi, i: (bi, i, 0)),
        out_shape=jax.ShapeDtypeStruct((b, s, 256), BF16),
        compiler_params=_cparams(("parallel", "arbitrary")),
        name="nsa_attn",
    )(q, cmp, cmp, ks, kv, gate, nw, consts["nsa_overlap"])


def _outmlp_kernel(h_ref, yg_ref, yn_ref, ys_ref, wog_ref, won_ref, wos_ref, n2_ref,
                   wup_ref, wdn_ref, fn_ref, o_ref, acc_ref, u_ref, *, final_norm):
    k = pl.program_id(1)

    @pl.when(k == 0)
    def _():
        h2 = (h_ref[...] + _dot(yg_ref[...], wog_ref[...]) + _dot(yn_ref[...], won_ref[...])
              + _dot(ys_ref[...], wos_ref[...]))
        acc_ref[...] = h2
        u = h2 * lax.rsqrt(jnp.mean(h2 * h2, axis=-1, keepdims=True) + NORM_EPS) * n2_ref[...]
        u_ref[...] = u.astype(BF16)

    a = jnp.maximum(_dot(u_ref[...], wup_ref[...]), 0.0)
    acc_ref[...] += _dot((a * a).astype(BF16), wdn_ref[...])

    @pl.when(k == pl.num_programs(1) - 1)
    def _():
        out = acc_ref[...]
        if final_norm:
            out = (out * lax.rsqrt(jnp.mean(out * out, axis=-1, keepdims=True) + NORM_EPS)
                   * fn_ref[...])
        o_ref[...] = out


def _outmlp(h2d, yg, yn, ys, wo, n2, wup, wdn, fn, layer, final_norm):
    m = h2d.shape[0]
    tm, tf = TM_MLP, TF_MLP
    row = lambda i, k: (i, 0)
    fixed = lambda i, k: (0, 0)
    return pl.pallas_call(
        functools.partial(_outmlp_kernel, final_norm=final_norm),
        grid=(m // tm, D_FF // tf),
        in_specs=[pl.BlockSpec((tm, D_MODEL), row),
                  pl.BlockSpec((tm, 256), row),
                  pl.BlockSpec((tm, 256), row),
                  pl.BlockSpec((tm, 512), row),
                  pl.BlockSpec((None, 256, D_MODEL), lambda i, k: (layer, 0, 0)),
                  pl.BlockSpec((None, 256, D_MODEL), lambda i, k: (layer, 1, 0)),
                  pl.BlockSpec((None, 512, D_MODEL), lambda i, k: (layer, 1, 0)),
                  pl.BlockSpec((1, D_MODEL), fixed),
                  pl.BlockSpec((None, D_MODEL, tf), lambda i, k: (layer, 0, k)),
                  pl.BlockSpec((None, tf, D_MODEL), lambda i, k: (layer, k, 0)),
                  pl.BlockSpec((1, D_MODEL), fixed)],
        out_specs=pl.BlockSpec((tm, D_MODEL), row),
        out_shape=jax.ShapeDtypeStruct((m, D_MODEL), F32),
        scratch_shapes=[pltpu.VMEM((tm, D_MODEL), F32), pltpu.VMEM((tm, D_MODEL), BF16)],
        compiler_params=_cparams(("parallel", "arbitrary")),
        name="outproj_mlp",
    )(h2d, yg, yn, ys, wo, wo, wo, n2, wup, wdn, fn)


def _constants(seq):
    c = {}
    i = np.arange(TT_GLA)
    c["gla_tri"] = ((i[:, None] >= i[None, :]) & (i[:, None] // GLA_SUB == i[None, :] // GLA_SUB))
    dk = np.arange(128) // GLA_DK
    dv = np.arange(256) // GLA_DV
    c["gla_hm"] = dk[:, None] == dv[None, :]
    c["gla_hm2"] = dv[:, None] == dk[None, :]
    c["gla_hmean"] = (dv[:, None] == dv[None, :]) / float(GLA_DV)
    i = np.arange(L_SSD)
    c["ssd_tri"] = i[:, None] >= i[None, :]
    hd = np.arange(128)
    c["ssd_ex"] = hd[:, None] == (np.arange(512) // 64)[None, :]
    nsb = seq // NSA_CMP_STRIDE
    cs = np.arange(nsb) * NSA_CMP_STRIDE
    ss = np.arange(LANES) * NSA_SEL_BLOCK
    c["nsa_overlap"] = ((cs[:, None] < ss[None, :] + NSA_SEL_BLOCK)
                        & (cs[:, None] + NSA_CMP_LEN > ss[None, :]))
    out = {k: jnp.asarray(np.asarray(v, np.float32), BF16) for k, v in c.items()}
    out["gla_hm2"] = out["gla_hm2"].astype(F32)
    return out


def _rope_tables(seq):
    half = NSA_DH // 2
    inv = ROPE_THETA ** (-jnp.arange(half, dtype=F32) / half)
    ang = jnp.arange(seq).astype(F32)[:, None] * inv[None, :]
    cos, sin = jnp.cos(ang), jnp.sin(ang)
    return jnp.tile(cos, (1, 4)), jnp.tile(jnp.concatenate([-sin, sin], axis=1), (1, 2))


def _pad_cols(w, width):
    return jnp.pad(w, ((0, 0), (0, 0), (0, width - w.shape[2])))


def _proj_weights(w_in):
    gq, gk, gv, glr, gr, nq, nkv, ngate, sz, sxbc, sdt = jnp.split(
        w_in.astype(BF16), np.cumsum(IN_SPLITS)[:-1].tolist(), axis=2)
    return dict(gla=jnp.concatenate([gq, gk, gv, gr, _pad_cols(glr, 128)], axis=2),
                q=nq, kvall=nkv, gate=_pad_cols(ngate, 128), z=sz, xbc=sxbc,
                dt=_pad_cols(sdt, 128))


def kernel(x, norm1_w, w_in, gla_gate_w2, gla_gate_b, gla_norm_w, nsa_cmp_pos_k, nsa_cmp_w1_k, nsa_cmp_w2_k, nsa_cmp_pos_v, nsa_cmp_w1_v, nsa_cmp_w2_v, nsa_norm_w, ssd_conv_w, ssd_conv_b, ssd_dt_bias, ssd_a_log, ssd_d, ssd_norm_w, w_out, norm2_w, w_up, w_down, final_norm_w):
    bsz, seq, _ = x.shape
    depth = w_in.shape[0]
    m = bsz * seq
    nsb = seq // NSA_CMP_STRIDE
    assert seq % 2048 == 0 and seq // NSA_SEL_BLOCK <= LANES
    consts = _constants(seq)
    cos, sin = _rope_tables(seq)
    h = x.reshape(m, D_MODEL)
    w_proj = _proj_weights(w_in)
    wo_b, wup_b, wdn_b = w_out.astype(BF16), w_up.astype(BF16), w_down.astype(BF16)
    for l in range(depth):
        p = _inproj(h, norm1_w[l][None, :], cos, sin, w_proj, l, seq)

        y_gla = _gla(p["gla"].reshape(bsz, seq, 896),
                     jnp.pad(gla_gate_w2[l], ((0, 112), (0, 0))).astype(BF16),
                     gla_gate_b[l][None, :], gla_norm_w[l][None, :], consts)

        y_ssd = _ssd(p["z"].reshape(bsz, seq, 512), p["xbc"].reshape(bsz, seq, 1024),
                     p["dt"].reshape(bsz, seq, 128), ssd_conv_w[l], ssd_conv_b[l][None, :],
                     jnp.pad(ssd_dt_bias[l], (0, 120))[None, :],
                     jnp.pad(ssd_a_log[l], (0, 120))[None, :],
                     jnp.repeat(ssd_d[l], 64)[None, :], ssd_norm_w[l][None, :], consts)

        pos = jnp.tile(jnp.stack([nsa_cmp_pos_k[l], nsa_cmp_pos_v[l]]), (1, 1, 2))
        w1 = jnp.stack([nsa_cmp_w1_k[l], nsa_cmp_w1_v[l]]).astype(BF16)
        w1 = _block_diag2(w1.reshape(2, NSA_CMP_LEN, NSA_DH, 128))
        w2 = _block_diag2(jnp.stack([nsa_cmp_w2_k[l], nsa_cmp_w2_v[l]]).astype(BF16))
        cmp = _compress(p["cmp"].reshape(bsz, seq, 256), pos, w1, w2)

        y_nsa = _nsa(p["q"].reshape(bsz, seq, 256), cmp,
                     p["ks"].reshape(2, bsz, seq, 128), p["kv"].reshape(bsz, seq, 384),
                     p["gate"].reshape(bsz, seq, 128),
                     nsa_norm_w[l][None, :], consts)

        h = _outmlp(h, y_gla.reshape(m, 256), y_nsa.reshape(m, 256), y_ssd.reshape(m, 512),
                    wo_b, norm2_w[l][None, :], wup_b, wdn_b, final_norm_w[None, :], l,
                    final_norm=(l == depth - 1))
    return h.reshape(bsz, seq, D_MODEL)
```

```python
import functools

import jax
import jax.numpy as jnp
import numpy as np
from jax import lax
from jax.experimental import pallas as pl
from jax.experimental.pallas import tpu as pltpu

F32 = jnp.float32
BF16 = jnp.bfloat16

D_MODEL = 1024
GLA_DK, GLA_DV = 32, 64
GLA_TAU = 16.0
GLA_SUB = 16
NSA_DH = 64
NSA_CMP_LEN, NSA_CMP_STRIDE = 32, 16
NSA_SEL_BLOCK, NSA_SEL_TOPK, NSA_SEL_LOCAL = 64, 16, 2
NSA_WINDOW = 512
SSD_D_INNER = 512
SSD_CONV = 4
SSD_CONV_DIM = 1024
D_FF = 4 * D_MODEL
ROPE_THETA = 10000.0
NORM_EPS = 1e-6
NEG = -1e30
BIG = 1e30
LANES = 128
VMEM_LIMIT = 56 * 1024 * 1024

IN_SPLITS = (128, 128, 256, 16, 256, 256, 768, 12, 512, SSD_CONV_DIM, 8)

TM_PROJ = 512
TT_GLA = 128
L_SSD = 128
TQ_NSA = 256
TK_NSA = 1024
NSA_FLAGS = TK_NSA // NSA_SEL_BLOCK
Q_SCALE = NSA_DH ** -0.5 * 1.4426950408889634
TM_MLP = 512
TF_MLP = 1024


def _cparams(sem):
    return pltpu.CompilerParams(dimension_semantics=sem, vmem_limit_bytes=VMEM_LIMIT)


def _dot(a, b):
    return jnp.dot(a, b, preferred_element_type=F32)


def _dot_nt(a, b):
    return lax.dot_general(a, b, (((1,), (1,)), ((), ())), preferred_element_type=F32)


def _split_dot_lhs(a, b_bf16, terms):
    acc, rem = None, a
    for _ in range(terms):
        hi = rem.astype(BF16)
        part = _dot(hi, b_bf16)
        acc = part if acc is None else acc + part
        rem = rem - hi.astype(F32)
    return acc


def _split_dot_rhs(a_bf16, b, terms):
    acc, rem = None, b
    for _ in range(terms):
        hi = rem.astype(BF16)
        part = _dot(a_bf16, hi)
        acc = part if acc is None else acc + part
        rem = rem - hi.astype(F32)
    return acc


def _sigmoid(x):
    return 1.0 / (1.0 + jnp.exp(-x))


def _softplus(x):
    return jnp.maximum(x, 0.0) + jnp.log1p(jnp.exp(-jnp.abs(x)))


def _rope_apply(x, cos, sin_signed, lane_lo):
    w = x.shape[-1]
    partner = jnp.where(lane_lo, pltpu.roll(x, w - 32, 1), pltpu.roll(x, 32, 1))
    return x * cos + partner * sin_signed


def _inproj_kernel(x_ref, nw_ref, cos_ref, sin_ref, wg_ref, wq_ref, wkv_ref, wgt_ref,
                   wz_ref, wx_ref, wdt_ref,
                   gla_ref, q_ref, cmp_ref, ks_ref, vs_ref, kw_ref, vw_ref, gate_ref, z_ref,
                   xbc_ref, dt_ref, *, tiles_per_seq):
    x = x_ref[...]
    u = x * lax.rsqrt(jnp.mean(x * x, axis=-1, keepdims=True) + NORM_EPS) * nw_ref[...]
    u = u.astype(BF16)
    gla_ref[...] = _dot(u, wg_ref[...])
    gate_ref[...] = _dot(u, wgt_ref[...])
    z_ref[...] = _dot(u, wz_ref[...])
    xbc_ref[...] = _dot(u, wx_ref[...])
    dt_ref[...] = _dot(u, wdt_ref[...])

    cos = cos_ref[...]
    sin = sin_ref[...]
    lane = lax.broadcasted_iota(jnp.int32, cos.shape, 1)
    lane_lo = (lane & 63) < 32
    q = _dot(u, wq_ref[...])
    for c in range(2):
        qc = _rope_apply(q[:, c * 128:(c + 1) * 128], cos, sin, lane_lo)
        q_ref[:, c * 128:(c + 1) * 128] = (qc * Q_SCALE).astype(BF16)
    kv = _dot(u, wkv_ref[...])
    cmp_ref[:, 0:128] = _rope_apply(kv[:, 0:128], cos, sin, lane_lo)
    cmp_ref[:, 128:256] = kv[:, 128:256]
    ksl = _rope_apply(kv[:, 256:384], cos, sin, lane_lo)
    tm = x.shape[0]
    pos = ((pl.program_id(0) % tiles_per_seq) * tm
           + lax.broadcasted_iota(jnp.int32, (tm, 1), 0))
    flag = jnp.where((lane - 64) == ((pos >> 6) & (NSA_FLAGS - 1)), BIG, 0.0)
    ks_ref[0] = jnp.where(lane < 64, ksl, flag).astype(BF16)
    ks_ref[1] = jnp.where(lane < 64, pltpu.roll(ksl, 64, 1), flag).astype(BF16)
    kw_ref[...] = _rope_apply(kv[:, 512:640], cos, sin, lane_lo).astype(BF16)
    one = jnp.where(lane == 64, 1.0, 0.0)
    for v_ref, vv in ((vs_ref, kv[:, 384:512]), (vw_ref, kv[:, 640:768])):
        v_ref[0] = jnp.where(lane < 64, vv, one).astype(BF16)
        v_ref[1] = jnp.where(lane < 64, pltpu.roll(vv, 64, 1), one).astype(BF16)


_PROJ_OUT = (("gla", 896, F32), ("q", 256, BF16), ("cmp", 256, F32), ("ks", 128, BF16),
             ("vs", 128, BF16), ("kw", 128, BF16), ("vw", 128, BF16), ("gate", 128, F32),
             ("z", 512, F32), ("xbc", 1024, F32), ("dt", 128, F32))
_PROJ_PER_GROUP = ("ks", "vs", "vw")
_PROJ_W = ("gla", "q", "kvall", "gate", "z", "xbc", "dt")


def _inproj(h2d, nw, cos, sin, w, layer, seq):
    m = h2d.shape[0]
    tm = TM_PROJ
    nt = seq // tm
    row = lambda i: (i, 0)
    fixed = lambda i: (0, 0)

    def ospec(name, wd):
        if name in _PROJ_PER_GROUP:
            return pl.BlockSpec((2, tm, wd), lambda i: (0, i, 0))
        return pl.BlockSpec((tm, wd), row)

    def oshape(name, wd, dt):
        return jax.ShapeDtypeStruct((2, m, wd) if name in _PROJ_PER_GROUP else (m, wd), dt)

    outs = pl.pallas_call(
        functools.partial(_inproj_kernel, tiles_per_seq=nt),
        grid=(m // tm,),
        in_specs=[pl.BlockSpec((tm, D_MODEL), row),
                  pl.BlockSpec((1, D_MODEL), fixed),
                  pl.BlockSpec((tm, LANES), lambda i: (i % nt, 0)),
                  pl.BlockSpec((tm, LANES), lambda i: (i % nt, 0))]
                 + [pl.BlockSpec((None,) + w[n].shape[1:], lambda i: (layer, 0, 0))
                    for n in _PROJ_W],
        out_specs=[ospec(n, wd) for n, wd, _ in _PROJ_OUT],
        out_shape=[oshape(n, wd, dt) for n, wd, dt in _PROJ_OUT],
        compiler_params=_cparams(("parallel",)),
        name="inproj",
    )(h2d, nw, cos, sin, *[w[n] for n in _PROJ_W])
    return {n: o for (n, _, _), o in zip(_PROJ_OUT, outs)}


def _gla_kernel(x_ref, w2_ref, bg_ref, nw_ref, tri_ref, hm_ref, hm2_ref, hmean_ref,
                o_ref, st_ref):
    @pl.when(pl.program_id(1) == 0)
    def _():
        st_ref[...] = jnp.zeros_like(st_ref)

    for s in range(x_ref.shape[0]):
        _gla_tile(s, x_ref, w2_ref, bg_ref, nw_ref, tri_ref, hm_ref, hm2_ref, hmean_ref,
                  o_ref, st_ref)


def _gla_tile(s, x_ref, w2_ref, bg_ref, nw_ref, tri_ref, hm_ref, hm2_ref, hmean_ref,
              o_ref, st_ref):
    tt = x_ref.shape[1]
    nsub = tt // GLA_SUB
    x = x_ref[s]
    q = x[:, 0:128] * (GLA_DK ** -0.5)
    k = x[:, 128:256]
    v = x[:, 256:512]
    r = x[:, 512:768]
    glr = x[:, 768:896].astype(BF16)
    pre = _dot(glr, w2_ref[...]) + bg_ref[...]
    log_a = -_softplus(-pre) * (1.0 / GLA_TAU)
    bc = _split_dot_rhs(tri_ref[...], log_a, 3)

    q3 = q.reshape(nsub, GLA_SUB, 128)
    k3 = k.reshape(nsub, GLA_SUB, 128)
    v3 = v.reshape(nsub, GLA_SUB, 256)
    bc3 = bc.reshape(nsub, GLA_SUB, 128)
    row = lax.broadcasted_iota(jnp.int32, (nsub, GLA_SUB, 128), 1)
    hm = hm_ref[...]

    o = jnp.zeros((tt, 256), F32)
    for j in range(GLA_SUB):
        diff = bc3 - bc3[:, j:j + 1, :]
        w = jnp.exp(jnp.where(row >= j, diff, NEG))
        t = (q3 * k3[:, j:j + 1, :] * w).reshape(tt, 128).astype(BF16)
        a = _dot(t, hm).reshape(nsub, GLA_SUB, 256)
        o = o + (a * v3[:, j:j + 1, :]).reshape(tt, 256)

    gtot3 = bc3[:, GLA_SUB - 1:GLA_SUB, :]
    kdec = (k3 * jnp.exp(gtot3 - bc3)).reshape(tt, 128).astype(BF16)
    qdec = q * jnp.exp(bc)
    bends = [bc[GLA_SUB - 1:GLA_SUB, :]]
    for c in range(1, nsub):
        bends.append(bends[-1] + bc[(c + 1) * GLA_SUB - 1:(c + 1) * GLA_SUB, :])
    bstart = jnp.concatenate(
        [jnp.zeros((GLA_SUB, 128), F32)]
        + [jnp.broadcast_to(bends[c], (GLA_SUB, 128)) for c in range(nsub - 1)], axis=0)
    st0 = st_ref[s]
    o = o + _dot_nt((qdec * jnp.exp(bstart)).astype(BF16), st0.astype(BF16))

    vt = v.T.astype(BF16)
    col = lax.broadcasted_iota(jnp.int32, (256, tt), 1)
    rowi = lax.broadcasted_iota(jnp.int32, (tt, 128), 0)
    hm2 = hm2_ref[...]
    st = st0 * jnp.exp(bends[nsub - 1])
    for b in range(nsub):
        lo = b * GLA_SUB
        vsel = jnp.where(col >= lo, jnp.where(col < lo + GLA_SUB, vt, 0.0), 0.0).astype(BF16)
        upd = _dot(vsel, kdec) * hm2
        if b < nsub - 1:
            later = jnp.exp(jnp.where(rowi >= lo + GLA_SUB, bstart - bends[b], NEG))
            o = o + _dot_nt((qdec * later).astype(BF16), upd.astype(BF16))
            st = st + upd * jnp.exp(bends[nsub - 1] - bends[b])
        else:
            st = st + upd
    st_ref[s] = st

    ms = _split_dot_lhs(o * o, hmean_ref[...], 2)
    y = o * lax.rsqrt(ms + NORM_EPS) * nw_ref[...]
    o_ref[s] = (y * (r * _sigmoid(r))).astype(o_ref.dtype)


def _gla(gla_in, w2, bg, nw, consts):
    b, s, _ = gla_in.shape
    tt = TT_GLA
    nb = 2 if b % 2 == 0 else 1
    fixed = lambda bi, i: (0, 0)
    return pl.pallas_call(
        _gla_kernel,
        grid=(b // nb, s // tt),
        in_specs=[pl.BlockSpec((nb, tt, 896), lambda bi, i: (bi, i, 0)),
                  pl.BlockSpec((128, 128), fixed),
                  pl.BlockSpec((1, 128), fixed),
                  pl.BlockSpec((1, 256), fixed),
                  pl.BlockSpec((tt, tt), fixed),
                  pl.BlockSpec((128, 256), fixed),
                  pl.BlockSpec((256, 128), fixed),
                  pl.BlockSpec((256, 256), fixed)],
        out_specs=pl.BlockSpec((nb, tt, 256), lambda bi, i: (bi, i, 0)),
        out_shape=jax.ShapeDtypeStruct((b, s, 256), BF16),
        scratch_shapes=[pltpu.VMEM((nb, 256, 128), F32)],
        compiler_params=_cparams(("parallel", "arbitrary")),
        name="gla",
    )(gla_in, w2, bg, nw, consts["gla_tri"], consts["gla_hm"], consts["gla_hm2"],
      consts["gla_hmean"])


def _ssd_kernel(z_ref, x_ref, halo_ref, dt_ref, cw_ref, cb_ref, dtb_ref, alog_ref,
                dskip_ref, nw_ref, tri_ref, ex_ref, o_ref, xe_ref, h_ref):
    L = x_ref.shape[1]
    i = pl.program_id(1)

    @pl.when(i == 0)
    def _():
        h_ref[...] = jnp.zeros_like(h_ref)

    halo = halo_ref[0]
    xe_ref[0:8, :] = jnp.where(i > 0, halo, jnp.zeros_like(halo))
    xe_ref[8:8 + L, :] = x_ref[0]
    cw = cw_ref[...]
    acc = jnp.broadcast_to(cb_ref[...], (L, SSD_CONV_DIM))
    for w in range(SSD_CONV):
        acc = acc + xe_ref[pl.ds(8 - (SSD_CONV - 1) + w, L), :] * cw[w:w + 1, :]
    xbc = acc * _sigmoid(acc)
    xs = xbc[:, 0:512]
    bm = xbc[:, 512:768]
    cm = xbc[:, 768:1024]

    dt = _softplus(dt_ref[0] + dtb_ref[...])
    a = dt * (-jnp.exp(alog_ref[...]))
    acs = _split_dot_rhs(tri_ref[...], a, 3)
    ex = ex_ref[...]
    dt_e = _split_dot_lhs(dt, ex, 3)
    acs_e = _split_dot_lhs(acs, ex, 3)
    acs_last_e = acs_e[L - 1:L, :]
    xdt = xs * dt_e
    xdt_b = xdt.astype(BF16)
    xd = (xdt * jnp.exp(acs_last_e - acs_e)).astype(BF16)
    eacs_e = jnp.exp(acs_e)
    cdec_e = jnp.exp(acs_last_e)
    acs_t = acs.T
    ri = lax.broadcasted_iota(jnp.int32, (L, L), 0)
    ci = lax.broadcasted_iota(jnp.int32, (L, L), 1)
    causal = ri >= ci

    ys = []
    for g in range(2):
        bg = bm[:, g * 128:(g + 1) * 128]
        cg = cm[:, g * 128:(g + 1) * 128].astype(BF16)
        cb = _dot_nt(cg, bg.astype(BF16))
        hg = h_ref[:, g * 256:(g + 1) * 256]
        yoff = _dot(cg, hg.astype(BF16)) * eacs_e[:, g * 256:(g + 1) * 256]
        ydiag = []
        for hh in range(4):
            hd = g * 4 + hh
            seg = jnp.exp(jnp.where(causal, acs[:, hd:hd + 1] - acs_t[hd:hd + 1, :], NEG))
            ydiag.append(_dot((cb * seg).astype(BF16), xdt_b[:, hd * 64:(hd + 1) * 64]))
        ys.append(yoff + jnp.concatenate(ydiag, axis=1))
        st = _dot(bg.T.astype(BF16), xd[:, g * 256:(g + 1) * 256])
        h_ref[:, g * 256:(g + 1) * 256] = hg * cdec_e[:, g * 256:(g + 1) * 256] + st

    y = jnp.concatenate(ys, axis=1) + dskip_ref[...] * xs
    zz = z_ref[0]
    y = y * (zz * _sigmoid(zz))
    nw = nw_ref[...]
    for g in range(2):
        yg = y[:, g * 256:(g + 1) * 256]
        ms = jnp.mean(yg * yg, axis=-1, keepdims=True)
        o_ref[0, :, g * 256:(g + 1) * 256] = (
            yg * lax.rsqrt(ms + NORM_EPS) * nw[:, g * 256:(g + 1) * 256]).astype(o_ref.dtype)


def _ssd(z, xbc, dt, cw, cb, dtb, alog, dskip_e, nw, consts):
    b, s, _ = z.shape
    L = L_SSD
    fixed = lambda bi, i: (0, 0)
    tok = lambda bi, i: (bi, i, 0)
    return pl.pallas_call(
        _ssd_kernel,
        grid=(b, s // L),
        in_specs=[pl.BlockSpec((1, L, 512), tok),
                  pl.BlockSpec((1, L, 1024), tok),
                  pl.BlockSpec((1, 8, 1024), lambda bi, i: (bi, jnp.maximum(i * (L // 8) - 1, 0), 0)),
                  pl.BlockSpec((1, L, 128), tok),
                  pl.BlockSpec((SSD_CONV, 1024), fixed),
                  pl.BlockSpec((1, 1024), fixed),
                  pl.BlockSpec((1, 128), fixed),
                  pl.BlockSpec((1, 128), fixed),
                  pl.BlockSpec((1, 512), fixed),
                  pl.BlockSpec((1, 512), fixed),
                  pl.BlockSpec((L, L), fixed),
                  pl.BlockSpec((128, 512), fixed)],
        out_specs=pl.BlockSpec((1, L, 512), tok),
        out_shape=jax.ShapeDtypeStruct((b, s, 512), BF16),
        scratch_shapes=[pltpu.VMEM((L + 8, 1024), F32), pltpu.VMEM((128, 512), F32)],
        compiler_params=_cparams(("parallel", "arbitrary")),
        name="ssd",
    )(z, xbc, xbc, dt, cw, cb, dtb, alog, dskip_e, nw, consts["ssd_tri"], consts["ssd_ex"])


def _cmp_kernel(x_ref, pos_ref, w1_ref, w2_ref, o_ref):
    nsb = o_ref.shape[2]
    pos = pos_ref[0]
    lo = jnp.zeros((nsb, 256), F32)
    hi = jnp.zeros((nsb, 256), F32)
    for t in range(NSA_CMP_STRIDE):
        xt = x_ref[0, pl.ds(t, nsb, stride=NSA_CMP_STRIDE), :]
        lo = lo + _dot((xt + pos[t:t + 1, :]).astype(BF16), w1_ref[0, t])
        u = t + NSA_CMP_STRIDE
        hi = hi + _dot((xt + pos[u:u + 1, :]).astype(BF16), w1_ref[0, u])
    hid = lo + pltpu.roll(hi, nsb - 1, 0)
    gl = 0.5 * hid * (1.0 + jnp.tanh(0.7978845608028654 * (hid + 0.044715 * hid * hid * hid)))
    o_ref[0, 0] = _dot(gl.astype(BF16), w2_ref[0]).astype(o_ref.dtype)


def _compress(x, pos, w1, w2):
    b, s, _ = x.shape
    nsb = s // NSA_CMP_STRIDE
    return pl.pallas_call(
        _cmp_kernel,
        grid=(b, 2),
        in_specs=[pl.BlockSpec((1, s, 128), lambda bi, c: (bi, 0, c)),
                  pl.BlockSpec((1, NSA_CMP_LEN, 128), lambda bi, c: (c, 0, 0)),
                  pl.BlockSpec((1, NSA_CMP_LEN, 128, 256), lambda bi, c: (c, 0, 0, 0)),
                  pl.BlockSpec((1, 256, 128), lambda bi, c: (c, 0, 0))],
        out_specs=pl.BlockSpec((1, 1, nsb, 128), lambda bi, c: (c, bi, 0, 0)),
        out_shape=jax.ShapeDtypeStruct((2, b, nsb, 128), BF16),
        compiler_params=_cparams(("parallel", "parallel")),
        name="nsa_compress",
    )(x, pos, w1, w2)


def _block_diag2(w):
    z = jnp.zeros_like(w)
    return jnp.concatenate([jnp.concatenate([w, z], axis=-1),
                            jnp.concatenate([z, w], axis=-1)], axis=-2)


def _nsa_kernel(q_ref, kc_ref, vc_ref, ks_ref, vs_ref, kw_ref, vw_ref, gate_ref, nw_ref, ov_ref,
                o_ref):
    tq = q_ref.shape[1]
    nsb = kc_ref.shape[1]
    tk = TK_NSA
    t0 = pl.program_id(1) * tq
    groups = (0, 1)

    lane = lax.broadcasted_iota(jnp.int32, (tq, LANES), 1)
    lane_lo = lane < 64
    gmask = (lane_lo, lane >= 64)
    tpos = t0 + lax.broadcasted_iota(jnp.int32, (tq, 1), 0)

    q2, qs = [], []
    for g in groups:
        qa = q_ref[0, :, g * 128:(g + 1) * 128].astype(F32)
        qb = pltpu.roll(qa, 64, 1)
        h0, h1 = (qa, qb) if g == 0 else (qb, qa)
        q2.append(jnp.concatenate([jnp.where(gmask[g], h0, 0.0), jnp.where(gmask[g], h1, 0.0)],
                                  axis=0).astype(BF16))
        qs.append((jnp.where(lane_lo, qa, 0.0), jnp.where(lane_lo, qb, 0.0)))

    nidx = lax.broadcasted_iota(jnp.int32, (tq, nsb), 1)
    cvalid = (nidx * NSA_CMP_STRIDE + (NSA_CMP_LEN - 1)) <= tpos
    def select(ncw, nb):
        o_cmp, imp_ts = [], []
        for g in groups:
            s = _dot_nt(q2[g], kc_ref[0, 0:ncw, :]).reshape(2, tq, ncw)
            s = jnp.where(cvalid[None, :, 0:ncw], s, NEG)
            e = jnp.where(cvalid[None, :, 0:ncw],
                          jnp.exp2(s - jnp.max(s, axis=-1, keepdims=True)), 0.0)
            den = jnp.sum(e, axis=-1, keepdims=True)
            p = e / jnp.where(den > 0.0, den, 1.0)
            oc = _dot(p.reshape(2 * tq, ncw).astype(BF16), vc_ref[0, 0:ncw, :])
            if g == 1:
                oc = pltpu.roll(oc, 64, 1)
            o_cmp.append(oc.reshape(2, tq, LANES))
            imp = _split_dot_lhs(p[0] + p[1], ov_ref[0:ncw, :], 2)
            imp_ts.append(imp.T[0:nb, :])

        imp_t = jnp.concatenate(imp_ts, axis=1)
        blk = lax.broadcasted_iota(jnp.int32, (nb, 2 * tq), 0)
        tcol = lax.broadcasted_iota(jnp.int32, (1, 2 * tq), 1)
        qblk = (t0 + jnp.where(tcol >= tq, tcol - tq, tcol)) >> 6
        forced = (blk == 0) | ((blk <= qblk) & (blk > qblk - NSA_SEL_LOCAL))
        future = blk > qblk
        blk_f = blk.astype(F32)

        def pick(_, work):
            mx = jnp.max(work, axis=0, keepdims=True)
            first = jnp.min(jnp.where(work == mx, blk_f, 1e9), axis=0, keepdims=True)
            return jnp.where(blk_f == first, -3e38, work)

        n_forced = 1 + NSA_SEL_LOCAL
        work = lax.fori_loop(0, NSA_SEL_TOPK - n_forced, pick,
                             jnp.where(forced, -3e38, jnp.where(future, NEG, imp_t)))
        veto_t = jnp.where(work < -2e38, jnp.where(future, -1.0, 0.0), -1.0)
        if nb < LANES:
            veto_t = jnp.concatenate([veto_t, jnp.full((LANES - nb, 2 * tq), -1.0, F32)], axis=0)
        return (o_cmp[0], o_cmp[1], veto_t[:, 0:tq].T, veto_t[:, tq:2 * tq].T)

    nclass = max(1, min(4, nsb // LANES))
    cw, bw = nsb // nclass, LANES // nclass
    size_class = jnp.minimum((t0 + tq - NSA_CMP_LEN) // (cw * NSA_CMP_STRIDE), nclass - 1)
    oc0, oc1, veto0, veto1 = lax.switch(
        size_class, [functools.partial(select, cw * (i + 1), bw * (i + 1)) for i in range(nclass)])
    o_cmp, veto = (oc0, oc1), (veto0, veto1)

    flag_lane = (lane >= 64) & (lane < 64 + NSA_FLAGS)

    def slc_step(c, width, carry, diagonal=False):
        k0 = pl.multiple_of(c * tk, tk)
        shift = (64 - NSA_FLAGS * c) & 127
        if diagonal:
            causal = (lax.broadcasted_iota(jnp.int32, (tq, width), 1)
                      <= lax.broadcasted_iota(jnp.int32, (tq, width), 0) + (width - tq))
        out = []
        for g in groups:
            m, acc = carry[g]
            ks = ks_ref[g, 0, pl.ds(k0, width), :]
            vs = vs_ref[g, 0, pl.ds(k0, width), :]
            aug = jnp.where(flag_lane, pltpu.roll(veto[g], shift, 1), 0.0)
            qq = jnp.concatenate([qs[g][0] + aug, qs[g][1] + aug], axis=0).astype(BF16)
            sc = _dot_nt(qq, ks).reshape(2, tq, width)
            if diagonal:
                sc = jnp.where(causal[None], sc, NEG)
            m_new = jnp.maximum(m, jnp.max(sc, axis=-1, keepdims=True))
            pe = jnp.exp2(sc - m_new).reshape(2 * tq, width).astype(BF16)
            pv = _dot(pe, vs).reshape(2, tq, LANES)
            out.append((m_new, jnp.exp2(m - m_new) * acc + pv))
        return tuple(out)

    init = (jnp.full((2, tq, 1), NEG, F32), jnp.zeros((2, tq, LANES), F32))
    nfull = t0 // tk
    carry = lax.fori_loop(0, nfull, lambda c, cr: slc_step(c, tk, cr), (init, init))
    carry = lax.switch(
        (t0 - nfull * tk) // tq,
        [functools.partial(slc_step, nfull, (j + 1) * tq, diagonal=True) for j in range(tk // tq)],
        carry)

    lw = tq + NSA_WINDOW
    w0 = pl.multiple_of(jnp.maximum(t0 - NSA_WINDOW, 0), tq)
    kw = kw_ref[0, pl.ds(w0, lw), :]
    dist = tpos - (w0 + lax.broadcasted_iota(jnp.int32, (tq, lw), 1))
    wvalid = jnp.where(dist >= 0, dist, NSA_WINDOW) < NSA_WINDOW

    sg = _sigmoid(gate_ref[0])
    for g in groups:
        acc_s = carry[g][1]
        o_slc = acc_s / acc_s[:, :, 64:65]
        sw = jnp.where(wvalid[None], _dot_nt(q2[g], kw).reshape(2, tq, lw), NEG)
        ew = jnp.exp2(sw - jnp.max(sw, axis=-1, keepdims=True))
        ow = _dot(ew.reshape(2 * tq, lw).astype(BF16), vw_ref[g, 0, pl.ds(w0, lw), :])
        ow = ow.reshape(2, tq, LANES)
        o_win = ow / ow[:, :, 64:65]
        outs = []
        for r in range(2):
            gcol = [jnp.sum(jnp.where(lane == g * 6 + r * 3 + c, sg, 0.0), axis=-1, keepdims=True)
                    for c in range(3)]
            o = gcol[0] * o_cmp[g][r] + gcol[1] * o_slc[r] + gcol[2] * o_win[r]
            ms = jnp.sum(jnp.where(lane_lo, o * o, 0.0), axis=-1, keepdims=True) * (1.0 / NSA_DH)
            outs.append(o * lax.rsqrt(ms + NORM_EPS))
        o_ref[0, :, g * 128:(g + 1) * 128] = (
            jnp.where(lane_lo, outs[0], pltpu.roll(outs[1], 64, 1))
            * nw_ref[:, g * 128:(g + 1) * 128]).astype(o_ref.dtype)


def _nsa(q, cmp, ks, vs, kw, vw, gate, nw, consts):
    b, s, _ = q.shape
    nsb = cmp.shape[2]
    tq = TQ_NSA
    return pl.pallas_call(
        _nsa_kernel,
        grid=(b, s // tq),
        in_specs=[pl.BlockSpec((1, tq, 256), lambda bi, i: (bi, i, 0)),
                  pl.BlockSpec((None, 1, nsb, 128), lambda bi, i: (0, bi, 0, 0)),
                  pl.BlockSpec((None, 1, nsb, 128), lambda bi, i: (1, bi, 0, 0)),
                  pl.BlockSpec((2, 1, s, 128), lambda bi, i: (0, bi, 0, 0)),
                  pl.BlockSpec((2, 1, s, 128), lambda bi, i: (0, bi, 0, 0)),
                  pl.BlockSpec((1, s, 128), lambda bi, i: (bi, 0, 0)),
                  pl.BlockSpec((2, 1, s, 128), lambda bi, i: (0, bi, 0, 0)),
                  pl.BlockSpec((1, tq, 128), lambda bi, i: (bi, i, 0)),
                  pl.BlockSpec((1, 256), lambda bi, i: (0, 0)),
                  pl.BlockSpec((nsb, 128), lambda bi, i: (0, 0))],
        out_specs=pl.BlockSpec((1, tq, 256), lambda bi, i: (bi, i, 0)),
        out_shape=jax.ShapeDtypeStruct((b, s, 256), BF16),
        compiler_params=_cparams(("parallel", "arbitrary")),
        name="nsa_attn",
    )(q, cmp, cmp, ks, vs, kw, vw, gate, nw, consts["nsa_overlap"])


def _outmlp_kernel(h_ref, yg_ref, yn_ref, ys_ref, wog_ref, won_ref, wos_ref, n2_ref,
                   wup_ref, wdn_ref, fn_ref, o_ref, acc_ref, u_ref, *, final_norm):
    k = pl.program_id(1)

    @pl.when(k == 0)
    def _():
        h2 = (h_ref[...] + _dot(yg_ref[...], wog_ref[...]) + _dot(yn_ref[...], won_ref[...])
              + _dot(ys_ref[...], wos_ref[...]))
        acc_ref[...] = h2
        u = h2 * lax.rsqrt(jnp.mean(h2 * h2, axis=-1, keepdims=True) + NORM_EPS) * n2_ref[...]
        u_ref[...] = u.astype(BF16)

    a = jnp.maximum(_dot(u_ref[...], wup_ref[...]), 0.0)
    acc_ref[...] += _dot((a * a).astype(BF16), wdn_ref[...])

    @pl.when(k == pl.num_programs(1) - 1)
    def _():
        out = acc_ref[...]
        if final_norm:
            out = (out * lax.rsqrt(jnp.mean(out * out, axis=-1, keepdims=True) + NORM_EPS)
                   * fn_ref[...])
        o_ref[...] = out


def _outmlp(h2d, yg, yn, ys, wo, n2, wup, wdn, fn, layer, final_norm):
    m = h2d.shape[0]
    tm, tf = TM_MLP, TF_MLP
    row = lambda i, k: (i, 0)
    fixed = lambda i, k: (0, 0)
    return pl.pallas_call(
        functools.partial(_outmlp_kernel, final_norm=final_norm),
        grid=(m // tm, D_FF // tf),
        in_specs=[pl.BlockSpec((tm, D_MODEL), row),
                  pl.BlockSpec((tm, 256), row),
                  pl.BlockSpec((tm, 256), row),
                  pl.BlockSpec((tm, 512), row),
                  pl.BlockSpec((None, 256, D_MODEL), lambda i, k: (layer, 0, 0)),
                  pl.BlockSpec((None, 256, D_MODEL), lambda i, k: (layer, 1, 0)),
                  pl.BlockSpec((None, 512, D_MODEL), lambda i, k: (layer, 1, 0)),
                  pl.BlockSpec((1, D_MODEL), fixed),
                  pl.BlockSpec((None, D_MODEL, tf), lambda i, k: (layer, 0, k)),
                  pl.BlockSpec((None, tf, D_MODEL), lambda i, k: (layer, k, 0)),
                  pl.BlockSpec((1, D_MODEL), fixed)],
        out_specs=pl.BlockSpec((tm, D_MODEL), row),
        out_shape=jax.ShapeDtypeStruct((m, D_MODEL), F32),
        scratch_shapes=[pltpu.VMEM((tm, D_MODEL), F32), pltpu.VMEM((tm, D_MODEL), BF16)],
        compiler_params=_cparams(("parallel", "arbitrary")),
        name="outproj_mlp",
    )(h2d, yg, yn, ys, wo, wo, wo, n2, wup, wdn, fn)


def _constants(seq):
    c = {}
    i = np.arange(TT_GLA)
    c["gla_tri"] = ((i[:, None] >= i[None, :]) & (i[:, None] // GLA_SUB == i[None, :] // GLA_SUB))
    dk = np.arange(128) // GLA_DK
    dv = np.arange(256) // GLA_DV
    c["gla_hm"] = dk[:, None] == dv[None, :]
    c["gla_hm2"] = dv[:, None] == dk[None, :]
    c["gla_hmean"] = (dv[:, None] == dv[None, :]) / float(GLA_DV)
    i = np.arange(L_SSD)
    c["ssd_tri"] = i[:, None] >= i[None, :]
    hd = np.arange(128)
    c["ssd_ex"] = hd[:, None] == (np.arange(512) // 64)[None, :]
    nsb = seq // NSA_CMP_STRIDE
    cs = np.arange(nsb) * NSA_CMP_STRIDE
    ss = np.arange(LANES) * NSA_SEL_BLOCK
    c["nsa_overlap"] = ((cs[:, None] < ss[None, :] + NSA_SEL_BLOCK)
                        & (cs[:, None] + NSA_CMP_LEN > ss[None, :]))
    out = {k: jnp.asarray(np.asarray(v, np.float32), BF16) for k, v in c.items()}
    out["gla_hm2"] = out["gla_hm2"].astype(F32)
    return out


def _rope_tables(seq):
    half = NSA_DH // 2
    inv = ROPE_THETA ** (-jnp.arange(half, dtype=F32) / half)
    ang = jnp.arange(seq).astype(F32)[:, None] * inv[None, :]
    cos, sin = jnp.cos(ang), jnp.sin(ang)
    return jnp.tile(cos, (1, 4)), jnp.tile(jnp.concatenate([-sin, sin], axis=1), (1, 2))


def _pad_cols(w, width):
    return jnp.pad(w, ((0, 0), (0, 0), (0, width - w.shape[2])))


def _proj_weights(w_in):
    gq, gk, gv, glr, gr, nq, nkv, ngate, sz, sxbc, sdt = jnp.split(
        w_in.astype(BF16), np.cumsum(IN_SPLITS)[:-1].tolist(), axis=2)
    return dict(gla=jnp.concatenate([gq, gk, gv, gr, _pad_cols(glr, 128)], axis=2),
                q=nq, kvall=nkv, gate=_pad_cols(ngate, 128), z=sz, xbc=sxbc,
                dt=_pad_cols(sdt, 128))


def kernel(x, norm1_w, w_in, gla_gate_w2, gla_gate_b, gla_norm_w, nsa_cmp_pos_k, nsa_cmp_w1_k, nsa_cmp_w2_k, nsa_cmp_pos_v, nsa_cmp_w1_v, nsa_cmp_w2_v, nsa_norm_w, ssd_conv_w, ssd_conv_b, ssd_dt_bias, ssd_a_log, ssd_d, ssd_norm_w, w_out, norm2_w, w_up, w_down, final_norm_w):
    bsz, seq, _ = x.shape
    depth = w_in.shape[0]
    m = bsz * seq
    nsb = seq // NSA_CMP_STRIDE
    assert seq % 2048 == 0 and seq // NSA_SEL_BLOCK <= LANES
    consts = _constants(seq)
    cos, sin = _rope_tables(seq)
    h = x.reshape(m, D_MODEL)
    w_proj = _proj_weights(w_in)
    wo_b, wup_b, wdn_b = w_out.astype(BF16), w_up.astype(BF16), w_down.astype(BF16)
    for l in range(depth):
        p = _inproj(h, norm1_w[l][None, :], cos, sin, w_proj, l, seq)

        y_gla = _gla(p["gla"].reshape(bsz, seq, 896),
                     jnp.pad(gla_gate_w2[l], ((0, 112), (0, 0))).astype(BF16),
                     gla_gate_b[l][None, :], gla_norm_w[l][None, :], consts)

        y_ssd = _ssd(p["z"].reshape(bsz, seq, 512), p["xbc"].reshape(bsz, seq, 1024),
                     p["dt"].reshape(bsz, seq, 128), ssd_conv_w[l], ssd_conv_b[l][None, :],
                     jnp.pad(ssd_dt_bias[l], (0, 120))[None, :],
                     jnp.pad(ssd_a_log[l], (0, 120))[None, :],
                     jnp.repeat(ssd_d[l], 64)[None, :], ssd_norm_w[l][None, :], consts)

        pos = jnp.tile(jnp.stack([nsa_cmp_pos_k[l], nsa_cmp_pos_v[l]]), (1, 1, 2))
        w1 = jnp.stack([nsa_cmp_w1_k[l], nsa_cmp_w1_v[l]]).astype(BF16)
        w1 = _block_diag2(w1.reshape(2, NSA_CMP_LEN, NSA_DH, 128))
        w2 = _block_diag2(jnp.stack([nsa_cmp_w2_k[l], nsa_cmp_w2_v[l]]).astype(BF16))
        cmp = _compress(p["cmp"].reshape(bsz, seq, 256), pos, w1, w2)

        y_nsa = _nsa(p["q"].reshape(bsz, seq, 256), cmp,
                     p["ks"].reshape(2, bsz, seq, 128), p["vs"].reshape(2, bsz, seq, 128),
                     p["kw"].reshape(bsz, seq, 128), p["vw"].reshape(2, bsz, seq, 128),
                     p["gate"].reshape(bsz, seq, 128),
                     nsa_norm_w[l][None, :], consts)

        h = _outmlp(h, y_gla.reshape(m, 256), y_nsa.reshape(m, 256), y_ssd.reshape(m, 512),
                    wo_b, norm2_w[l][None, :], wup_b, wdn_b, final_norm_w[None, :], l,
                    final_norm=(l == depth - 1))
    return h.reshape(bsz, seq, D_MODEL)
```

```python
import functools

import jax
import jax.numpy as jnp
import numpy as np
from jax import lax
from jax.experimental import pallas as pl
from jax.experimental.pallas import tpu as pltpu

F32 = jnp.float32
BF16 = jnp.bfloat16

D_MODEL = 1024
GLA_DK, GLA_DV = 32, 64
GLA_TAU = 16.0
GLA_SUB = 16
NSA_DH = 64
NSA_CMP_LEN, NSA_CMP_STRIDE = 32, 16
NSA_SEL_BLOCK, NSA_SEL_TOPK, NSA_SEL_LOCAL = 64, 16, 2
NSA_WINDOW = 512
SSD_D_INNER = 512
SSD_CONV = 4
SSD_CONV_DIM = 1024
D_FF = 4 * D_MODEL
ROPE_THETA = 10000.0
NORM_EPS = 1e-6
NEG = -1e30
BIG = 1e30
LANES = 128
VMEM_LIMIT = 56 * 1024 * 1024

IN_SPLITS = (128, 128, 256, 16, 256, 256, 768, 12, 512, SSD_CONV_DIM, 8)

TM_PROJ = 512
TT_GLA = 128
L_SSD = 128
TQ_NSA = 256
TK_NSA = 1024
NSA_FLAGS = TK_NSA // NSA_SEL_BLOCK
Q_SCALE = NSA_DH ** -0.5 * 1.4426950408889634
TM_MLP = 512
TF_MLP = 4096


def _cparams(sem):
    return pltpu.CompilerParams(dimension_semantics=sem, vmem_limit_bytes=VMEM_LIMIT)


def _dot(a, b):
    return jnp.dot(a, b, preferred_element_type=F32)


def _dot_nt(a, b):
    return lax.dot_general(a, b, (((1,), (1,)), ((), ())), preferred_element_type=F32)


def _split_dot_lhs(a, b_bf16, terms):
    acc, rem = None, a
    for _ in range(terms):
        hi = rem.astype(BF16)
        part = _dot(hi, b_bf16)
        acc = part if acc is None else acc + part
        rem = rem - hi.astype(F32)
    return acc


def _split_dot_rhs(a_bf16, b, terms):
    acc, rem = None, b
    for _ in range(terms):
        hi = rem.astype(BF16)
        part = _dot(a_bf16, hi)
        acc = part if acc is None else acc + part
        rem = rem - hi.astype(F32)
    return acc


def _sigmoid(x):
    return 1.0 / (1.0 + jnp.exp(-x))


def _softplus(x):
    return jnp.maximum(x, 0.0) + jnp.log1p(jnp.exp(-jnp.abs(x)))


def _rope_apply(x, cos, sin_signed, lane_lo):
    w = x.shape[-1]
    partner = jnp.where(lane_lo, pltpu.roll(x, w - 32, 1), pltpu.roll(x, 32, 1))
    return x * cos + partner * sin_signed


def _inproj_kernel(x_ref, nw_ref, cos_ref, sin_ref, wa_ref, wb_ref, wkv_ref,
                   gla_ref, q_ref, cmp_ref, ks_ref, vs_ref, kw_ref, vw_ref, gate_ref, z_ref,
                   xbc_ref, dt_ref, *, tiles_per_seq):
    tm = x_ref.shape[0]
    half = tm // 2
    for lo in (0, half):
        rows = slice(lo, lo + half)
        x = x_ref[rows, :]
        u = x * lax.rsqrt(jnp.mean(x * x, axis=-1, keepdims=True) + NORM_EPS) * nw_ref[...]
        u = u.astype(BF16)
        pa = _dot(u, wa_ref[...])
        gla_ref[rows, :] = pa[:, 0:896]
        gate_ref[rows, :] = pa[:, 896:1024]
        pb = _dot(u, wb_ref[...])
        z_ref[rows, :] = pb[:, 0:512]
        xbc_ref[rows, :] = pb[:, 512:1536]
        dt_ref[rows, :] = pb[:, 1536:1664]

        cos = cos_ref[rows, :]
        sin = sin_ref[rows, :]
        lane = lax.broadcasted_iota(jnp.int32, cos.shape, 1)
        lane_lo = (lane & 63) < 32
        q = pb[:, 1664:1920]
        for c in range(2):
            qc = _rope_apply(q[:, c * 128:(c + 1) * 128], cos, sin, lane_lo)
            q_ref[rows, c * 128:(c + 1) * 128] = (qc * Q_SCALE).astype(BF16)
        kv = _dot(u, wkv_ref[...])
        cmp_ref[rows, 0:128] = _rope_apply(kv[:, 0:128], cos, sin, lane_lo)
        cmp_ref[rows, 128:256] = kv[:, 128:256]
        ksl = _rope_apply(kv[:, 256:384], cos, sin, lane_lo)
        pos = ((pl.program_id(0) % tiles_per_seq) * tm + lo
               + lax.broadcasted_iota(jnp.int32, (half, 1), 0))
        flag = jnp.where((lane - 64) == ((pos >> 6) & (NSA_FLAGS - 1)), BIG, 0.0)
        ks_ref[0, rows, :] = jnp.where(lane < 64, ksl, flag).astype(BF16)
        ks_ref[1, rows, :] = jnp.where(lane < 64, pltpu.roll(ksl, 64, 1), flag).astype(BF16)
        kw_ref[rows, :] = _rope_apply(kv[:, 512:640], cos, sin, lane_lo).astype(BF16)
        one = jnp.where(lane == 64, 1.0, 0.0)
        for v_ref, vv in ((vs_ref, kv[:, 384:512]), (vw_ref, kv[:, 640:768])):
            v_ref[0, rows, :] = jnp.where(lane < 64, vv, one).astype(BF16)
            v_ref[1, rows, :] = jnp.where(lane < 64, pltpu.roll(vv, 64, 1), one).astype(BF16)


_PROJ_OUT = (("gla", 896, F32), ("q", 256, BF16), ("cmp", 256, F32), ("ks", 128, BF16),
             ("vs", 128, BF16), ("kw", 128, BF16), ("vw", 128, BF16), ("gate", 128, F32),
             ("z", 512, F32), ("xbc", 1024, F32), ("dt", 128, F32))
_PROJ_PER_GROUP = ("ks", "vs", "vw")
_PROJ_W = ("a", "b", "kvall")


def _inproj(h2d, nw, cos, sin, w, layer, seq):
    m = h2d.shape[0]
    tm = TM_PROJ
    nt = seq // tm
    row = lambda i: (i, 0)
    fixed = lambda i: (0, 0)

    def ospec(name, wd):
        if name in _PROJ_PER_GROUP:
            return pl.BlockSpec((2, tm, wd), lambda i: (0, i, 0))
        return pl.BlockSpec((tm, wd), row)

    def oshape(name, wd, dt):
        return jax.ShapeDtypeStruct((2, m, wd) if name in _PROJ_PER_GROUP else (m, wd), dt)

    outs = pl.pallas_call(
        functools.partial(_inproj_kernel, tiles_per_seq=nt),
        grid=(m // tm,),
        in_specs=[pl.BlockSpec((tm, D_MODEL), row),
                  pl.BlockSpec((1, D_MODEL), fixed),
                  pl.BlockSpec((tm, LANES), lambda i: (i % nt, 0)),
                  pl.BlockSpec((tm, LANES), lambda i: (i % nt, 0))]
                 + [pl.BlockSpec((None,) + w[n].shape[1:], lambda i: (layer, 0, 0))
                    for n in _PROJ_W],
        out_specs=[ospec(n, wd) for n, wd, _ in _PROJ_OUT],
        out_shape=[oshape(n, wd, dt) for n, wd, dt in _PROJ_OUT],
        compiler_params=_cparams(("parallel",)),
        name="inproj",
    )(h2d, nw, cos, sin, *[w[n] for n in _PROJ_W])
    return {n: o for (n, _, _), o in zip(_PROJ_OUT, outs)}


def _gla_kernel(x_ref, w2_ref, bg_ref, nw_ref, tri_ref, hm_ref, hm2_ref, hmean_ref,
                o_ref, st_ref):
    @pl.when(pl.program_id(1) == 0)
    def _():
        st_ref[...] = jnp.zeros_like(st_ref)

    for s in range(x_ref.shape[0]):
        _gla_tile(s, x_ref, w2_ref, bg_ref, nw_ref, tri_ref, hm_ref, hm2_ref, hmean_ref,
                  o_ref, st_ref)


def _gla_tile(s, x_ref, w2_ref, bg_ref, nw_ref, tri_ref, hm_ref, hm2_ref, hmean_ref,
              o_ref, st_ref):
    tt = x_ref.shape[1]
    nsub = tt // GLA_SUB
    x = x_ref[s]
    q = x[:, 0:128] * (GLA_DK ** -0.5)
    k = x[:, 128:256]
    v = x[:, 256:512]
    r = x[:, 512:768]
    glr = x[:, 768:896].astype(BF16)
    pre = _dot(glr, w2_ref[...]) + bg_ref[...]
    log_a = -_softplus(-pre) * (1.0 / GLA_TAU)
    bc = _split_dot_rhs(tri_ref[...], log_a, 3)

    q3 = q.reshape(nsub, GLA_SUB, 128)
    k3 = k.reshape(nsub, GLA_SUB, 128)
    v3 = v.reshape(nsub, GLA_SUB, 256)
    bc3 = bc.reshape(nsub, GLA_SUB, 128)
    row = lax.broadcasted_iota(jnp.int32, (nsub, GLA_SUB, 128), 1)
    hm = hm_ref[...]

    o = jnp.zeros((tt, 256), F32)
    for j in range(GLA_SUB):
        diff = bc3 - bc3[:, j:j + 1, :]
        w = jnp.exp(jnp.where(row >= j, diff, NEG))
        t = (q3 * k3[:, j:j + 1, :] * w).reshape(tt, 128).astype(BF16)
        a = _dot(t, hm).reshape(nsub, GLA_SUB, 256)
        o = o + (a * v3[:, j:j + 1, :]).reshape(tt, 256)

    gtot3 = bc3[:, GLA_SUB - 1:GLA_SUB, :]
    kdec = (k3 * jnp.exp(gtot3 - bc3)).reshape(tt, 128).astype(BF16)
    qdec = q * jnp.exp(bc)
    bends = [bc[GLA_SUB - 1:GLA_SUB, :]]
    for c in range(1, nsub):
        bends.append(bends[-1] + bc[(c + 1) * GLA_SUB - 1:(c + 1) * GLA_SUB, :])
    bstart = jnp.concatenate(
        [jnp.zeros((GLA_SUB, 128), F32)]
        + [jnp.broadcast_to(bends[c], (GLA_SUB, 128)) for c in range(nsub - 1)], axis=0)
    st0 = st_ref[s]
    o = o + _dot_nt((qdec * jnp.exp(bstart)).astype(BF16), st0.astype(BF16))

    vt = v.T.astype(BF16)
    col = lax.broadcasted_iota(jnp.int32, (256, tt), 1)
    rowi = lax.broadcasted_iota(jnp.int32, (tt, 128), 0)
    hm2 = hm2_ref[...]
    st = st0 * jnp.exp(bends[nsub - 1])
    for b in range(nsub):
        lo = b * GLA_SUB
        vsel = jnp.where(col >= lo, jnp.where(col < lo + GLA_SUB, vt, 0.0), 0.0).astype(BF16)
        upd = _dot(vsel, kdec) * hm2
        if b < nsub - 1:
            later = jnp.exp(jnp.where(rowi >= lo + GLA_SUB, bstart - bends[b], NEG))
            o = o + _dot_nt((qdec * later).astype(BF16), upd.astype(BF16))
            st = st + upd * jnp.exp(bends[nsub - 1] - bends[b])
        else:
            st = st + upd
    st_ref[s] = st

    ms = _split_dot_lhs(o * o, hmean_ref[...], 2)
    y = o * lax.rsqrt(ms + NORM_EPS) * nw_ref[...]
    o_ref[s] = (y * (r * _sigmoid(r))).astype(o_ref.dtype)


def _gla(gla_in, w2, bg, nw, consts):
    b, s, _ = gla_in.shape
    tt = TT_GLA
    nb = next(n for n in (4, 2, 1) if b % n == 0)
    fixed = lambda bi, i: (0, 0)
    return pl.pallas_call(
        _gla_kernel,
        grid=(b // nb, s // tt),
        in_specs=[pl.BlockSpec((nb, tt, 896), lambda bi, i: (bi, i, 0)),
                  pl.BlockSpec((128, 128), fixed),
                  pl.BlockSpec((1, 128), fixed),
                  pl.BlockSpec((1, 256), fixed),
                  pl.BlockSpec((tt, tt), fixed),
                  pl.BlockSpec((128, 256), fixed),
                  pl.BlockSpec((256, 128), fixed),
                  pl.BlockSpec((256, 256), fixed)],
        out_specs=pl.BlockSpec((nb, tt, 256), lambda bi, i: (bi, i, 0)),
        out_shape=jax.ShapeDtypeStruct((b, s, 256), BF16),
        scratch_shapes=[pltpu.VMEM((nb, 256, 128), F32)],
        compiler_params=_cparams(("parallel", "arbitrary")),
        name="gla",
    )(gla_in, w2, bg, nw, consts["gla_tri"], consts["gla_hm"], consts["gla_hm2"],
      consts["gla_hmean"])


def _ssd_kernel(z_ref, x_ref, halo_ref, dt_ref, cw_ref, cb_ref, dtb_ref, alog_ref,
                dskip_ref, nw_ref, tri_ref, ex_ref, o_ref, xe_ref, h_ref):
    L = x_ref.shape[1]
    i = pl.program_id(1)

    @pl.when(i == 0)
    def _():
        h_ref[...] = jnp.zeros_like(h_ref)

    halo = halo_ref[0]
    xe_ref[0:8, :] = jnp.where(i > 0, halo, jnp.zeros_like(halo))
    xe_ref[8:8 + L, :] = x_ref[0]
    cw = cw_ref[...]
    acc = jnp.broadcast_to(cb_ref[...], (L, SSD_CONV_DIM))
    for w in range(SSD_CONV):
        acc = acc + xe_ref[pl.ds(8 - (SSD_CONV - 1) + w, L), :] * cw[w:w + 1, :]
    xbc = acc * _sigmoid(acc)
    xs = xbc[:, 0:512]
    bm = xbc[:, 512:768]
    cm = xbc[:, 768:1024]

    dt = _softplus(dt_ref[0] + dtb_ref[...])
    a = dt * (-jnp.exp(alog_ref[...]))
    acs = _split_dot_rhs(tri_ref[...], a, 3)
    ex = ex_ref[...]
    dt_e = _split_dot_lhs(dt, ex, 3)
    acs_e = _split_dot_lhs(acs, ex, 3)
    acs_last_e = acs_e[L - 1:L, :]
    xdt = xs * dt_e
    xdt_b = xdt.astype(BF16)
    xd = (xdt * jnp.exp(acs_last_e - acs_e)).astype(BF16)
    eacs_e = jnp.exp(acs_e)
    cdec_e = jnp.exp(acs_last_e)
    acs_t = acs.T
    ri = lax.broadcasted_iota(jnp.int32, (L, L), 0)
    ci = lax.broadcasted_iota(jnp.int32, (L, L), 1)
    causal = ri >= ci

    ys = []
    for g in range(2):
        bg = bm[:, g * 128:(g + 1) * 128]
        cg = cm[:, g * 128:(g + 1) * 128].astype(BF16)
        cb = _dot_nt(cg, bg.astype(BF16))
        hg = h_ref[:, g * 256:(g + 1) * 256]
        yoff = _dot(cg, hg.astype(BF16)) * eacs_e[:, g * 256:(g + 1) * 256]
        ydiag = []
        for hh in range(4):
            hd = g * 4 + hh
            seg = jnp.exp(jnp.where(causal, acs[:, hd:hd + 1] - acs_t[hd:hd + 1, :], NEG))
            ydiag.append(_dot((cb * seg).astype(BF16), xdt_b[:, hd * 64:(hd + 1) * 64]))
        ys.append(yoff + jnp.concatenate(ydiag, axis=1))
        st = _dot(bg.T.astype(BF16), xd[:, g * 256:(g + 1) * 256])
        h_ref[:, g * 256:(g + 1) * 256] = hg * cdec_e[:, g * 256:(g + 1) * 256] + st

    y = jnp.concatenate(ys, axis=1) + dskip_ref[...] * xs
    zz = z_ref[0]
    y = y * (zz * _sigmoid(zz))
    nw = nw_ref[...]
    for g in range(2):
        yg = y[:, g * 256:(g + 1) * 256]
        ms = jnp.mean(yg * yg, axis=-1, keepdims=True)
        o_ref[0, :, g * 256:(g + 1) * 256] = (
            yg * lax.rsqrt(ms + NORM_EPS) * nw[:, g * 256:(g + 1) * 256]).astype(o_ref.dtype)


def _ssd(z, xbc, dt, cw, cb, dtb, alog, dskip_e, nw, consts):
    b, s, _ = z.shape
    L = L_SSD
    fixed = lambda bi, i: (0, 0)
    tok = lambda bi, i: (bi, i, 0)
    return pl.pallas_call(
        _ssd_kernel,
        grid=(b, s // L),
        in_specs=[pl.BlockSpec((1, L, 512), tok),
                  pl.BlockSpec((1, L, 1024), tok),
                  pl.BlockSpec((1, 8, 1024), lambda bi, i: (bi, jnp.maximum(i * (L // 8) - 1, 0), 0)),
                  pl.BlockSpec((1, L, 128), tok),
                  pl.BlockSpec((SSD_CONV, 1024), fixed),
                  pl.BlockSpec((1, 1024), fixed),
                  pl.BlockSpec((1, 128), fixed),
                  pl.BlockSpec((1, 128), fixed),
                  pl.BlockSpec((1, 512), fixed),
                  pl.BlockSpec((1, 512), fixed),
                  pl.BlockSpec((L, L), fixed),
                  pl.BlockSpec((128, 512), fixed)],
        out_specs=pl.BlockSpec((1, L, 512), tok),
        out_shape=jax.ShapeDtypeStruct((b, s, 512), BF16),
        scratch_shapes=[pltpu.VMEM((L + 8, 1024), F32), pltpu.VMEM((128, 512), F32)],
        compiler_params=_cparams(("parallel", "arbitrary")),
        name="ssd",
    )(z, xbc, xbc, dt, cw, cb, dtb, alog, dskip_e, nw, consts["ssd_tri"], consts["ssd_ex"])


def _cmp_kernel(x_ref, pos_ref, w1_ref, w2_ref, o_ref):
    nsb = o_ref.shape[2]
    pos = pos_ref[0]
    lo = jnp.zeros((nsb, 256), F32)
    hi = jnp.zeros((nsb, 256), F32)
    for t in range(NSA_CMP_STRIDE):
        xt = x_ref[0, pl.ds(t, nsb, stride=NSA_CMP_STRIDE), :]
        lo = lo + _dot((xt + pos[t:t + 1, :]).astype(BF16), w1_ref[0, t])
        u = t + NSA_CMP_STRIDE
        hi = hi + _dot((xt + pos[u:u + 1, :]).astype(BF16), w1_ref[0, u])
    hid = lo + pltpu.roll(hi, nsb - 1, 0)
    gl = 0.5 * hid * (1.0 + jnp.tanh(0.7978845608028654 * (hid + 0.044715 * hid * hid * hid)))
    o_ref[0, 0] = _dot(gl.astype(BF16), w2_ref[0]).astype(o_ref.dtype)


def _compress(x, pos, w1, w2):
    b, s, _ = x.shape
    nsb = s // NSA_CMP_STRIDE
    return pl.pallas_call(
        _cmp_kernel,
        grid=(b, 2),
        in_specs=[pl.BlockSpec((1, s, 128), lambda bi, c: (bi, 0, c)),
                  pl.BlockSpec((1, NSA_CMP_LEN, 128), lambda bi, c: (c, 0, 0)),
                  pl.BlockSpec((1, NSA_CMP_LEN, 128, 256), lambda bi, c: (c, 0, 0, 0)),
                  pl.BlockSpec((1, 256, 128), lambda bi, c: (c, 0, 0))],
        out_specs=pl.BlockSpec((1, 1, nsb, 128), lambda bi, c: (c, bi, 0, 0)),
        out_shape=jax.ShapeDtypeStruct((2, b, nsb, 128), BF16),
        compiler_params=_cparams(("parallel", "parallel")),
        name="nsa_compress",
    )(x, pos, w1, w2)


def _block_diag2(w):
    z = jnp.zeros_like(w)
    return jnp.concatenate([jnp.concatenate([w, z], axis=-1),
                            jnp.concatenate([z, w], axis=-1)], axis=-2)


def _nsa_kernel(q_ref, kc_ref, vc_ref, ks_ref, vs_ref, kw_ref, vw_ref, gate_ref, nw_ref, ov_ref,
                o_ref):
    tq = q_ref.shape[1]
    nsb = kc_ref.shape[1]
    tk = TK_NSA
    t0 = pl.program_id(1) * tq
    groups = (0, 1)

    lane = lax.broadcasted_iota(jnp.int32, (tq, LANES), 1)
    lane_lo = lane < 64
    gmask = (lane_lo, lane >= 64)
    tpos = t0 + lax.broadcasted_iota(jnp.int32, (tq, 1), 0)

    q2, qs = [], []
    for g in groups:
        qa = q_ref[0, :, g * 128:(g + 1) * 128].astype(F32)
        qb = pltpu.roll(qa, 64, 1)
        h0, h1 = (qa, qb) if g == 0 else (qb, qa)
        q2.append(jnp.concatenate([jnp.where(gmask[g], h0, 0.0), jnp.where(gmask[g], h1, 0.0)],
                                  axis=0).astype(BF16))
        qs.append((jnp.where(lane_lo, qa, 0.0), jnp.where(lane_lo, qb, 0.0)))

    nidx = lax.broadcasted_iota(jnp.int32, (tq, nsb), 1)
    cvalid = (nidx * NSA_CMP_STRIDE + (NSA_CMP_LEN - 1)) <= tpos

    def select(ncw, nb):
        o_cmp, imp_ts = [], []
        for g in groups:
            s = _dot_nt(q2[g], kc_ref[0, 0:ncw, :]).reshape(2, tq, ncw)
            s = jnp.where(cvalid[None, :, 0:ncw], s, NEG)
            e = jnp.where(cvalid[None, :, 0:ncw],
                          jnp.exp2(s - jnp.max(s, axis=-1, keepdims=True)), 0.0)
            den = jnp.sum(e, axis=-1, keepdims=True)
            p = e / jnp.where(den > 0.0, den, 1.0)
            oc = _dot(p.reshape(2 * tq, ncw).astype(BF16), vc_ref[0, 0:ncw, :])
            if g == 1:
                oc = pltpu.roll(oc, 64, 1)
            o_cmp.append(oc.reshape(2, tq, LANES))
            imp = _split_dot_lhs(p[0] + p[1], ov_ref[0:ncw, :], 2)
            imp_ts.append(imp.T[0:nb, :])

        imp_t = jnp.concatenate(imp_ts, axis=1)
        blk = lax.broadcasted_iota(jnp.int32, (nb, 2 * tq), 0)
        tcol = lax.broadcasted_iota(jnp.int32, (1, 2 * tq), 1)
        qblk = (t0 + jnp.where(tcol >= tq, tcol - tq, tcol)) >> 6
        forced = (blk == 0) | ((blk <= qblk) & (blk > qblk - NSA_SEL_LOCAL))
        future = blk > qblk
        blk_f = blk.astype(F32)

        def pick(_, work):
            mx = jnp.max(work, axis=0, keepdims=True)
            first = jnp.min(jnp.where(work == mx, blk_f, 1e9), axis=0, keepdims=True)
            return jnp.where(blk_f == first, -3e38, work)

        n_forced = 1 + NSA_SEL_LOCAL
        work = lax.fori_loop(0, NSA_SEL_TOPK - n_forced, pick,
                             jnp.where(forced, -3e38, jnp.where(future, NEG, imp_t)))
        veto_t = jnp.where(work < -2e38, jnp.where(future, -1.0, 0.0), -1.0)
        if nb < LANES:
            veto_t = jnp.concatenate([veto_t, jnp.full((LANES - nb, 2 * tq), -1.0, F32)], axis=0)
        return (o_cmp[0], o_cmp[1], veto_t[:, 0:tq].T, veto_t[:, tq:2 * tq].T)

    nclass = max(1, min(4, nsb // LANES))
    cw, bw = nsb // nclass, LANES // nclass
    size_class = jnp.minimum((t0 + tq - NSA_CMP_LEN) // (cw * NSA_CMP_STRIDE), nclass - 1)
    oc0, oc1, veto0, veto1 = lax.switch(
        size_class, [functools.partial(select, cw * (i + 1), bw * (i + 1)) for i in range(nclass)])
    o_cmp, veto = (oc0, oc1), (veto0, veto1)

    flag_lane = (lane >= 64) & (lane < 64 + NSA_FLAGS)

    def slc_step(c, width, carry, diagonal=False):
        k0 = pl.multiple_of(c * tk, tk)
        shift = (64 - NSA_FLAGS * c) & 127
        if diagonal:
            causal = (lax.broadcasted_iota(jnp.int32, (tq, width), 1)
                      <= lax.broadcasted_iota(jnp.int32, (tq, width), 0) + (width - tq))
        out = []
        for g in groups:
            m, acc = carry[g]
            ks = ks_ref[g, 0, pl.ds(k0, width), :]
            vs = vs_ref[g, 0, pl.ds(k0, width), :]
            aug = jnp.where(flag_lane, pltpu.roll(veto[g], shift, 1), 0.0)
            qq = jnp.concatenate([qs[g][0] + aug, qs[g][1] + aug], axis=0).astype(BF16)
            sc = _dot_nt(qq, ks).reshape(2, tq, width)
            if diagonal:
                sc = jnp.where(causal[None], sc, NEG)
            m_new = jnp.maximum(m, jnp.max(sc, axis=-1, keepdims=True))
            pe = jnp.exp2(sc - m_new).reshape(2 * tq, width).astype(BF16)
            pv = _dot(pe, vs).reshape(2, tq, LANES)
            out.append((m_new, jnp.exp2(m - m_new) * acc + pv))
        return tuple(out)

    init = (jnp.full((2, tq, 1), NEG, F32), jnp.zeros((2, tq, LANES), F32))
    nfull = t0 // tk
    carry = lax.fori_loop(
        0, nfull // 2, lambda c, cr: slc_step(2 * c + 1, tk, slc_step(2 * c, tk, cr)),
        (init, init))
    carry = lax.cond(nfull % 2 == 1, lambda cr: slc_step(nfull - 1, tk, cr), lambda cr: cr, carry)
    carry = lax.switch(
        (t0 - nfull * tk) // tq,
        [functools.partial(slc_step, nfull, (j + 1) * tq, diagonal=True) for j in range(tk // tq)],
        carry)

    lw = tq + NSA_WINDOW
    w0 = pl.multiple_of(jnp.maximum(t0 - NSA_WINDOW, 0), tq)
    kw = kw_ref[0, pl.ds(w0, lw), :]
    dist = tpos - (w0 + lax.broadcasted_iota(jnp.int32, (tq, lw), 1))
    wvalid = jnp.where(dist >= 0, dist, NSA_WINDOW) < NSA_WINDOW

    sg = _sigmoid(gate_ref[0])
    for g in groups:
        acc_s = carry[g][1]
        o_slc = acc_s / acc_s[:, :, 64:65]
        sw = jnp.where(wvalid[None], _dot_nt(q2[g], kw).reshape(2, tq, lw), NEG)
        ew = jnp.exp2(sw - jnp.max(sw, axis=-1, keepdims=True))
        ow = _dot(ew.reshape(2 * tq, lw).astype(BF16), vw_ref[g, 0, pl.ds(w0, lw), :])
        ow = ow.reshape(2, tq, LANES)
        o_win = ow / ow[:, :, 64:65]
        outs = []
        for r in range(2):
            gcol = [jnp.sum(jnp.where(lane == g * 6 + r * 3 + c, sg, 0.0), axis=-1, keepdims=True)
                    for c in range(3)]
            o = gcol[0] * o_cmp[g][r] + gcol[1] * o_slc[r] + gcol[2] * o_win[r]
            ms = jnp.sum(jnp.where(lane_lo, o * o, 0.0), axis=-1, keepdims=True) * (1.0 / NSA_DH)
            outs.append(o * lax.rsqrt(ms + NORM_EPS))
        o_ref[0, :, g * 128:(g + 1) * 128] = (
            jnp.where(lane_lo, outs[0], pltpu.roll(outs[1], 64, 1))
            * nw_ref[:, g * 128:(g + 1) * 128]).astype(o_ref.dtype)


def _nsa(q, cmp, ks, vs, kw, vw, gate, nw, consts):
    b, s, _ = q.shape
    nsb = cmp.shape[2]
    tq = TQ_NSA
    return pl.pallas_call(
        _nsa_kernel,
        grid=(b, s // tq),
        in_specs=[pl.BlockSpec((1, tq, 256), lambda bi, i: (bi, i, 0)),
                  pl.BlockSpec((None, 1, nsb, 128), lambda bi, i: (0, bi, 0, 0)),
                  pl.BlockSpec((None, 1, nsb, 128), lambda bi, i: (1, bi, 0, 0)),
                  pl.BlockSpec((2, 1, s, 128), lambda bi, i: (0, bi, 0, 0)),
                  pl.BlockSpec((2, 1, s, 128), lambda bi, i: (0, bi, 0, 0)),
                  pl.BlockSpec((1, s, 128), lambda bi, i: (bi, 0, 0)),
                  pl.BlockSpec((2, 1, s, 128), lambda bi, i: (0, bi, 0, 0)),
                  pl.BlockSpec((1, tq, 128), lambda bi, i: (bi, i, 0)),
                  pl.BlockSpec((1, 256), lambda bi, i: (0, 0)),
                  pl.BlockSpec((nsb, 128), lambda bi, i: (0, 0))],
        out_specs=pl.BlockSpec((1, tq, 256), lambda bi, i: (bi, i, 0)),
        out_shape=jax.ShapeDtypeStruct((b, s, 256), BF16),
        compiler_params=_cparams(("parallel", "arbitrary")),
        name="nsa_attn",
    )(q, cmp, cmp, ks, vs, kw, vw, gate, nw, consts["nsa_overlap"])


def _outmlp_kernel(h_ref, yg_ref, yn_ref, ys_ref, wog_ref, won_ref, wos_ref, n2_ref,
                   wup_ref, wdn_ref, fn_ref, o_ref, acc_ref, u_ref, *, final_norm):
    k = pl.program_id(1)

    @pl.when(k == 0)
    def _():
        h2 = (h_ref[...] + _dot(yg_ref[...], wog_ref[...]) + _dot(yn_ref[...], won_ref[...])
              + _dot(ys_ref[...], wos_ref[...]))
        acc_ref[...] = h2
        u = h2 * lax.rsqrt(jnp.mean(h2 * h2, axis=-1, keepdims=True) + NORM_EPS) * n2_ref[...]
        u_ref[...] = u.astype(BF16)

    a = jnp.maximum(_dot(u_ref[...], wup_ref[...]), 0.0)
    acc_ref[...] += _dot((a * a).astype(BF16), wdn_ref[...])

    @pl.when(k == pl.num_programs(1) - 1)
    def _():
        out = acc_ref[...]
        if final_norm:
            out = (out * lax.rsqrt(jnp.mean(out * out, axis=-1, keepdims=True) + NORM_EPS)
                   * fn_ref[...])
        o_ref[...] = out


def _outmlp(h2d, yg, yn, ys, wo, n2, wup, wdn, fn, layer, final_norm):
    m = h2d.shape[0]
    tm, tf = TM_MLP, TF_MLP
    row = lambda i, k: (i, 0)
    fixed = lambda i, k: (0, 0)
    return pl.pallas_call(
        functools.partial(_outmlp_kernel, final_norm=final_norm),
        grid=(m // tm, D_FF // tf),
        in_specs=[pl.BlockSpec((tm, D_MODEL), row),
                  pl.BlockSpec((tm, 256), row),
                  pl.BlockSpec((tm, 256), row),
                  pl.BlockSpec((tm, 512), row),
                  pl.BlockSpec((None, 256, D_MODEL), lambda i, k: (layer, 0, 0)),
                  pl.BlockSpec((None, 256, D_MODEL), lambda i, k: (layer, 1, 0)),
                  pl.BlockSpec((None, 512, D_MODEL), lambda i, k: (layer, 1, 0)),
                  pl.BlockSpec((1, D_MODEL), fixed),
                  pl.BlockSpec((None, D_MODEL, tf), lambda i, k: (layer, 0, k)),
                  pl.BlockSpec((None, tf, D_MODEL), lambda i, k: (layer, k, 0)),
                  pl.BlockSpec((1, D_MODEL), fixed)],
        out_specs=pl.BlockSpec((tm, D_MODEL), row),
        out_shape=jax.ShapeDtypeStruct((m, D_MODEL), F32),
        scratch_shapes=[pltpu.VMEM((tm, D_MODEL), F32), pltpu.VMEM((tm, D_MODEL), BF16)],
        compiler_params=_cparams(("parallel", "arbitrary")),
        name="outproj_mlp",
    )(h2d, yg, yn, ys, wo, wo, wo, n2, wup, wdn, fn)


def _constants(seq):
    c = {}
    i = np.arange(TT_GLA)
    c["gla_tri"] = ((i[:, None] >= i[None, :]) & (i[:, None] // GLA_SUB == i[None, :] // GLA_SUB))
    dk = np.arange(128) // GLA_DK
    dv = np.arange(256) // GLA_DV
    c["gla_hm"] = dk[:, None] == dv[None, :]
    c["gla_hm2"] = dv[:, None] == dk[None, :]
    c["gla_hmean"] = (dv[:, None] == dv[None, :]) / float(GLA_DV)
    i = np.arange(L_SSD)
    c["ssd_tri"] = i[:, None] >= i[None, :]
    hd = np.arange(128)
    c["ssd_ex"] = hd[:, None] == (np.arange(512) // 64)[None, :]
    nsb = seq // NSA_CMP_STRIDE
    cs = np.arange(nsb) * NSA_CMP_STRIDE
    ss = np.arange(LANES) * NSA_SEL_BLOCK
    c["nsa_overlap"] = ((cs[:, None] < ss[None, :] + NSA_SEL_BLOCK)
                        & (cs[:, None] + NSA_CMP_LEN > ss[None, :]))
    out = {k: jnp.asarray(np.asarray(v, np.float32), BF16) for k, v in c.items()}
    out["gla_hm2"] = out["gla_hm2"].astype(F32)
    return out


def _rope_tables(seq):
    half = NSA_DH // 2
    inv = ROPE_THETA ** (-jnp.arange(half, dtype=F32) / half)
    ang = jnp.arange(seq).astype(F32)[:, None] * inv[None, :]
    cos, sin = jnp.cos(ang), jnp.sin(ang)
    return jnp.tile(cos, (1, 4)), jnp.tile(jnp.concatenate([-sin, sin], axis=1), (1, 2))


def _pad_cols(w, width):
    return jnp.pad(w, ((0, 0), (0, 0), (0, width - w.shape[2])))


def _proj_weights(w_in):
    gq, gk, gv, glr, gr, nq, nkv, ngate, sz, sxbc, sdt = jnp.split(
        w_in.astype(BF16), np.cumsum(IN_SPLITS)[:-1].tolist(), axis=2)
    return dict(
        a=jnp.concatenate([gq, gk, gv, gr, _pad_cols(glr, 128), _pad_cols(ngate, 128)], axis=2),
        b=jnp.concatenate([sz, sxbc, _pad_cols(sdt, 128), nq], axis=2),
        kvall=nkv)


def kernel(x, norm1_w, w_in, gla_gate_w2, gla_gate_b, gla_norm_w, nsa_cmp_pos_k, nsa_cmp_w1_k, nsa_cmp_w2_k, nsa_cmp_pos_v, nsa_cmp_w1_v, nsa_cmp_w2_v, nsa_norm_w, ssd_conv_w, ssd_conv_b, ssd_dt_bias, ssd_a_log, ssd_d, ssd_norm_w, w_out, norm2_w, w_up, w_down, final_norm_w):
    bsz, seq, _ = x.shape
    depth = w_in.shape[0]
    m = bsz * seq
    nsb = seq // NSA_CMP_STRIDE
    assert seq % 2048 == 0 and seq // NSA_SEL_BLOCK <= LANES
    consts = _constants(seq)
    cos, sin = _rope_tables(seq)
    h = x.reshape(m, D_MODEL)
    w_proj = _proj_weights(w_in)
    wo_b, wup_b, wdn_b = w_out.astype(BF16), w_up.astype(BF16), w_down.astype(BF16)
    for l in range(depth):
        p = _inproj(h, norm1_w[l][None, :], cos, sin, w_proj, l, seq)

        y_gla = _gla(p["gla"].reshape(bsz, seq, 896),
                     jnp.pad(gla_gate_w2[l], ((0, 112), (0, 0))).astype(BF16),
                     gla_gate_b[l][None, :], gla_norm_w[l][None, :], consts)

        y_ssd = _ssd(p["z"].reshape(bsz, seq, 512), p["xbc"].reshape(bsz, seq, 1024),
                     p["dt"].reshape(bsz, seq, 128), ssd_conv_w[l], ssd_conv_b[l][None, :],
                     jnp.pad(ssd_dt_bias[l], (0, 120))[None, :],
                     jnp.pad(ssd_a_log[l], (0, 120))[None, :],
                     jnp.repeat(ssd_d[l], 64)[None, :], ssd_norm_w[l][None, :], consts)

        pos = jnp.tile(jnp.stack([nsa_cmp_pos_k[l], nsa_cmp_pos_v[l]]), (1, 1, 2))
        w1 = jnp.stack([nsa_cmp_w1_k[l], nsa_cmp_w1_v[l]]).astype(BF16)
        w1 = _block_diag2(w1.reshape(2, NSA_CMP_LEN, NSA_DH, 128))
        w2 = _block_diag2(jnp.stack([nsa_cmp_w2_k[l], nsa_cmp_w2_v[l]]).astype(BF16))
        cmp = _compress(p["cmp"].reshape(bsz, seq, 256), pos, w1, w2)

        y_nsa = _nsa(p["q"].reshape(bsz, seq, 256), cmp,
                     p["ks"].reshape(2, bsz, seq, 128), p["vs"].reshape(2, bsz, seq, 128),
                     p["kw"].reshape(bsz, seq, 128), p["vw"].reshape(2, bsz, seq, 128),
                     p["gate"].reshape(bsz, seq, 128),
                     nsa_norm_w[l][None, :], consts)

        h = _outmlp(h, y_gla.reshape(m, 256), y_nsa.reshape(m, 256), y_ssd.reshape(m, 512),
                    wo_b, norm2_w[l][None, :], wup_b, wdn_b, final_norm_w[None, :], l,
                    final_norm=(l == depth - 1))
    return h.reshape(bsz, seq, D_MODEL)
```

```python
import functools

import jax
import jax.numpy as jnp
import numpy as np
from jax import lax
from jax.experimental import pallas as pl
from jax.experimental.pallas import tpu as pltpu

F32 = jnp.float32
BF16 = jnp.bfloat16

D_MODEL = 1024
GLA_DK, GLA_DV = 32, 64
GLA_TAU = 16.0
GLA_SUB = 16
NSA_DH = 64
NSA_CMP_LEN, NSA_CMP_STRIDE = 32, 16
NSA_SEL_BLOCK, NSA_SEL_TOPK, NSA_SEL_LOCAL = 64, 16, 2
NSA_WINDOW = 512
SSD_D_INNER = 512
SSD_CONV = 4
SSD_CONV_DIM = 1024
D_FF = 4 * D_MODEL
ROPE_THETA = 10000.0
NORM_EPS = 1e-6
NEG = -1e30
BIG = 1e30
LANES = 128
VMEM_LIMIT = 56 * 1024 * 1024

IN_SPLITS = (128, 128, 256, 16, 256, 256, 768, 12, 512, SSD_CONV_DIM, 8)

TM_PROJ = 512
TT_GLA = 128
L_SSD = 128
TQ_NSA = 256
TK_NSA = 1024
NSA_FLAGS = TK_NSA // NSA_SEL_BLOCK
Q_SCALE = NSA_DH ** -0.5 * 1.4426950408889634
TM_MLP = 512
TF_MLP = 4096


def _cparams(sem):
    return pltpu.CompilerParams(dimension_semantics=sem, vmem_limit_bytes=VMEM_LIMIT)


def _dot(a, b):
    return jnp.dot(a, b, preferred_element_type=F32)


def _dot_nt(a, b):
    return lax.dot_general(a, b, (((1,), (1,)), ((), ())), preferred_element_type=F32)


def _split_dot_lhs(a, b_bf16, terms):
    acc, rem = None, a
    for _ in range(terms):
        hi = rem.astype(BF16)
        part = _dot(hi, b_bf16)
        acc = part if acc is None else acc + part
        rem = rem - hi.astype(F32)
    return acc


def _split_dot_rhs(a_bf16, b, terms):
    acc, rem = None, b
    for _ in range(terms):
        hi = rem.astype(BF16)
        part = _dot(a_bf16, hi)
        acc = part if acc is None else acc + part
        rem = rem - hi.astype(F32)
    return acc


def _sigmoid(x):
    return 1.0 / (1.0 + jnp.exp(-x))


def _softplus(x):
    return jnp.maximum(x, 0.0) + jnp.log1p(jnp.exp(-jnp.abs(x)))


def _rope_apply(x, cos, sin_signed, lane_lo):
    w = x.shape[-1]
    partner = jnp.where(lane_lo, pltpu.roll(x, w - 32, 1), pltpu.roll(x, 32, 1))
    return x * cos + partner * sin_signed


def _inproj_kernel(x_ref, xprev_ref, nw_ref, cos_ref, sin_ref, wa_ref, wb_ref, wkv_ref,
                   cw_ref, cb_ref,
                   gla_ref, q_ref, cmp_ref, ks_ref, vs_ref, kw_ref, vw_ref, gate_ref, z_ref,
                   xbc_ref, dt_ref, xe_ref, *, tiles_per_seq):
    tm = x_ref.shape[0]
    half = tm // 2
    halo = xprev_ref.shape[0]
    seq_start = (pl.program_id(0) % tiles_per_seq) == 0
    cw = cw_ref[...]

    def normed(x):
        y = x * lax.rsqrt(jnp.mean(x * x, axis=-1, keepdims=True) + NORM_EPS) * nw_ref[...]
        return y.astype(BF16)

    for part, lo in enumerate((0, half)):
        rows = slice(lo, lo + half)
        u = normed(x_ref[rows, :])
        if part == 0:
            pb = _dot(jnp.concatenate([normed(xprev_ref[...]), u], axis=0), wb_ref[...])
            before = jnp.where(seq_start, 0.0, pb[0:halo, 512:1536])
            pb = pb[halo:, :]
        else:
            pb = _dot(u, wb_ref[...])
            before = raw_tail
        raw = pb[:, 512:1536]
        raw_tail = raw[half - halo:, :]
        z_ref[rows, :] = pb[:, 0:512]
        dt_ref[rows, :] = pb[:, 1536:1664]
        xe_ref[part, 0:halo, :] = before
        xe_ref[part, halo:halo + half, :] = raw
        conv = jnp.broadcast_to(cb_ref[...], (half, SSD_CONV_DIM))
        for w in range(SSD_CONV):
            conv = conv + xe_ref[part, pl.ds(halo - (SSD_CONV - 1) + w, half), :] * cw[w:w + 1, :]
        xbc_ref[rows, :] = conv * _sigmoid(conv)

        cos = cos_ref[rows, :]
        sin = sin_ref[rows, :]
        lane = lax.broadcasted_iota(jnp.int32, cos.shape, 1)
        lane_lo = (lane & 63) < 32
        q = pb[:, 1664:1920]
        for c in range(2):
            qc = _rope_apply(q[:, c * 128:(c + 1) * 128], cos, sin, lane_lo)
            q_ref[rows, c * 128:(c + 1) * 128] = (qc * Q_SCALE).astype(BF16)
        kv = _dot(u, wkv_ref[...])
        cmp_ref[rows, 0:128] = _rope_apply(kv[:, 0:128], cos, sin, lane_lo)
        cmp_ref[rows, 128:256] = kv[:, 128:256]
        ksl = _rope_apply(kv[:, 256:384], cos, sin, lane_lo)
        pos = ((pl.program_id(0) % tiles_per_seq) * tm + lo
               + lax.broadcasted_iota(jnp.int32, (half, 1), 0))
        flag = jnp.where((lane - 64) == ((pos >> 6) & (NSA_FLAGS - 1)), BIG, 0.0)
        ks_ref[0, rows, :] = jnp.where(lane < 64, ksl, flag).astype(BF16)
        ks_ref[1, rows, :] = jnp.where(lane < 64, pltpu.roll(ksl, 64, 1), flag).astype(BF16)
        kw_ref[rows, :] = _rope_apply(kv[:, 512:640], cos, sin, lane_lo).astype(BF16)
        one = jnp.where(lane == 64, 1.0, 0.0)
        for v_ref, vv in ((vs_ref, kv[:, 384:512]), (vw_ref, kv[:, 640:768])):
            v_ref[0, rows, :] = jnp.where(lane < 64, vv, one).astype(BF16)
            v_ref[1, rows, :] = jnp.where(lane < 64, pltpu.roll(vv, 64, 1), one).astype(BF16)
        pa = _dot(u, wa_ref[...])
        gla_ref[rows, :] = pa[:, 0:896]
        gate_ref[rows, :] = pa[:, 896:1024]


_PROJ_OUT = (("gla", 896, F32), ("q", 256, BF16), ("cmp", 256, F32), ("ks", 128, BF16),
             ("vs", 128, BF16), ("kw", 128, BF16), ("vw", 128, BF16), ("gate", 128, F32),
             ("z", 512, F32), ("xbc", 1024, F32), ("dt", 128, F32))
_PROJ_PER_GROUP = ("ks", "vs", "vw")
_PROJ_W = ("a", "b", "kvall")


def _inproj(h2d, nw, cos, sin, w, cw, cb, layer, seq):
    m = h2d.shape[0]
    tm = TM_PROJ
    nt = seq // tm
    halo = 16
    row = lambda i: (i, 0)
    fixed = lambda i: (0, 0)

    def ospec(name, wd):
        if name in _PROJ_PER_GROUP:
            return pl.BlockSpec((2, tm, wd), lambda i: (0, i, 0))
        return pl.BlockSpec((tm, wd), row)

    def oshape(name, wd, dt):
        return jax.ShapeDtypeStruct((2, m, wd) if name in _PROJ_PER_GROUP else (m, wd), dt)

    outs = pl.pallas_call(
        functools.partial(_inproj_kernel, tiles_per_seq=nt),
        grid=(m // tm,),
        in_specs=[pl.BlockSpec((tm, D_MODEL), row),
                  pl.BlockSpec((halo, D_MODEL),
                               lambda i: (jnp.maximum(i * (tm // halo) - 1, 0), 0)),
                  pl.BlockSpec((1, D_MODEL), fixed),
                  pl.BlockSpec((tm, LANES), lambda i: (i % nt, 0)),
                  pl.BlockSpec((tm, LANES), lambda i: (i % nt, 0))]
                 + [pl.BlockSpec((None,) + w[n].shape[1:], lambda i: (layer, 0, 0))
                    for n in _PROJ_W]
                 + [pl.BlockSpec((SSD_CONV, SSD_CONV_DIM), fixed),
                    pl.BlockSpec((1, SSD_CONV_DIM), fixed)],
        out_specs=[ospec(n, wd) for n, wd, _ in _PROJ_OUT],
        out_shape=[oshape(n, wd, dt) for n, wd, dt in _PROJ_OUT],
        scratch_shapes=[pltpu.VMEM((2, tm // 2 + halo, SSD_CONV_DIM), F32)],
        compiler_params=_cparams(("parallel",)),
        name="inproj",
    )(h2d, h2d, nw, cos, sin, *[w[n] for n in _PROJ_W], cw, cb)
    return {n: o for (n, _, _), o in zip(_PROJ_OUT, outs)}


def _gla_kernel(x_ref, w2_ref, bg_ref, nw_ref, tri_ref, hm_ref, hm2_ref, hmean_ref,
                o_ref, st_ref):
    @pl.when(pl.program_id(1) == 0)
    def _():
        st_ref[...] = jnp.zeros_like(st_ref)

    for s in range(x_ref.shape[0]):
        _gla_tile(s, x_ref, w2_ref, bg_ref, nw_ref, tri_ref, hm_ref, hm2_ref, hmean_ref,
                  o_ref, st_ref)


def _gla_tile(s, x_ref, w2_ref, bg_ref, nw_ref, tri_ref, hm_ref, hm2_ref, hmean_ref,
              o_ref, st_ref):
    tt = x_ref.shape[1]
    nsub = tt // GLA_SUB
    x = x_ref[s]
    q = x[:, 0:128] * (GLA_DK ** -0.5)
    k = x[:, 128:256]
    v = x[:, 256:512]
    r = x[:, 512:768]
    glr = x[:, 768:896].astype(BF16)
    pre = _dot(glr, w2_ref[...]) + bg_ref[...]
    log_a = -_softplus(-pre) * (1.0 / GLA_TAU)
    bc = _split_dot_rhs(tri_ref[...], log_a, 3)

    q3 = q.reshape(nsub, GLA_SUB, 128)
    k3 = k.reshape(nsub, GLA_SUB, 128)
    v3 = v.reshape(nsub, GLA_SUB, 256)
    bc3 = bc.reshape(nsub, GLA_SUB, 128)
    row = lax.broadcasted_iota(jnp.int32, (nsub, GLA_SUB, 128), 1)
    hm = hm_ref[...]

    o = jnp.zeros((tt, 256), F32)
    for j in range(GLA_SUB):
        diff = bc3 - bc3[:, j:j + 1, :]
        w = jnp.exp(jnp.where(row >= j, diff, NEG))
        t = (q3 * k3[:, j:j + 1, :] * w).reshape(tt, 128).astype(BF16)
        a = _dot(t, hm).reshape(nsub, GLA_SUB, 256)
        o = o + (a * v3[:, j:j + 1, :]).reshape(tt, 256)

    gtot3 = bc3[:, GLA_SUB - 1:GLA_SUB, :]
    kdec = (k3 * jnp.exp(gtot3 - bc3)).reshape(tt, 128).astype(BF16)
    qdec = q * jnp.exp(bc)
    bends = [bc[GLA_SUB - 1:GLA_SUB, :]]
    for c in range(1, nsub):
        bends.append(bends[-1] + bc[(c + 1) * GLA_SUB - 1:(c + 1) * GLA_SUB, :])
    bstart = jnp.concatenate(
        [jnp.zeros((GLA_SUB, 128), F32)]
        + [jnp.broadcast_to(bends[c], (GLA_SUB, 128)) for c in range(nsub - 1)], axis=0)
    st0 = st_ref[s]
    o = o + _dot_nt((qdec * jnp.exp(bstart)).astype(BF16), st0.astype(BF16))

    vt = v.T.astype(BF16)
    col = lax.broadcasted_iota(jnp.int32, (256, tt), 1)
    rowi = lax.broadcasted_iota(jnp.int32, (tt, 128), 0)
    hm2 = hm2_ref[...]
    st = st0 * jnp.exp(bends[nsub - 1])
    for b in range(nsub):
        lo = b * GLA_SUB
        vsel = jnp.where(col >= lo, jnp.where(col < lo + GLA_SUB, vt, 0.0), 0.0).astype(BF16)
        upd = _dot(vsel, kdec) * hm2
        if b < nsub - 1:
            later = jnp.exp(jnp.where(rowi >= lo + GLA_SUB, bstart - bends[b], NEG))
            o = o + _dot_nt((qdec * later).astype(BF16), upd.astype(BF16))
            st = st + upd * jnp.exp(bends[nsub - 1] - bends[b])
        else:
            st = st + upd
    st_ref[s] = st

    ms = _split_dot_lhs(o * o, hmean_ref[...], 2)
    y = o * lax.rsqrt(ms + NORM_EPS) * nw_ref[...]
    o_ref[s] = (y * (r * _sigmoid(r))).astype(o_ref.dtype)


def _gla(gla_in, w2, bg, nw, consts):
    b, s, _ = gla_in.shape
    tt = TT_GLA
    nb = next(n for n in (4, 2, 1) if b % n == 0)
    fixed = lambda bi, i: (0, 0)
    return pl.pallas_call(
        _gla_kernel,
        grid=(b // nb, s // tt),
        in_specs=[pl.BlockSpec((nb, tt, 896), lambda bi, i: (bi, i, 0)),
                  pl.BlockSpec((128, 128), fixed),
                  pl.BlockSpec((1, 128), fixed),
                  pl.BlockSpec((1, 256), fixed),
                  pl.BlockSpec((tt, tt), fixed),
                  pl.BlockSpec((128, 256), fixed),
                  pl.BlockSpec((256, 128), fixed),
                  pl.BlockSpec((256, 256), fixed)],
        out_specs=pl.BlockSpec((nb, tt, 256), lambda bi, i: (bi, i, 0)),
        out_shape=jax.ShapeDtypeStruct((b, s, 256), BF16),
        scratch_shapes=[pltpu.VMEM((nb, 256, 128), F32)],
        compiler_params=_cparams(("parallel", "arbitrary")),
        name="gla",
    )(gla_in, w2, bg, nw, consts["gla_tri"], consts["gla_hm"], consts["gla_hm2"],
      consts["gla_hmean"])


def _ssd_kernel(z_ref, x_ref, dt_ref, dtb_ref, alog_ref,
                dskip_ref, nw_ref, tri_ref, ex_ref, o_ref, h_ref):
    @pl.when(pl.program_id(1) == 0)
    def _():
        h_ref[...] = jnp.zeros_like(h_ref)

    for s in range(x_ref.shape[0]):
        _ssd_chunk(s, z_ref, x_ref, dt_ref, dtb_ref, alog_ref, dskip_ref, nw_ref, tri_ref, ex_ref,
                   o_ref, h_ref)


def _ssd_chunk(s, z_ref, x_ref, dt_ref, dtb_ref, alog_ref, dskip_ref, nw_ref, tri_ref, ex_ref,
               o_ref, h_ref):
    L = x_ref.shape[1]
    xbc = x_ref[s]
    xs = xbc[:, 0:512]
    bm = xbc[:, 512:768]
    cm = xbc[:, 768:1024]

    dt = _softplus(dt_ref[s] + dtb_ref[...])
    a = dt * (-jnp.exp(alog_ref[...]))
    acs = _split_dot_rhs(tri_ref[...], a, 3)
    ex = ex_ref[...]
    dt_e = _split_dot_lhs(dt, ex, 3)
    acs_e = _split_dot_lhs(acs, ex, 3)
    acs_last_e = acs_e[L - 1:L, :]
    xdt = xs * dt_e
    xdt_b = xdt.astype(BF16)
    xd = (xdt * jnp.exp(acs_last_e - acs_e)).astype(BF16)
    eacs_e = jnp.exp(acs_e)
    cdec_e = jnp.exp(acs_last_e)
    acs_t = acs.T
    ri = lax.broadcasted_iota(jnp.int32, (L, L), 0)
    ci = lax.broadcasted_iota(jnp.int32, (L, L), 1)
    causal = ri >= ci

    ys = []
    for g in range(2):
        bg = bm[:, g * 128:(g + 1) * 128]
        cg = cm[:, g * 128:(g + 1) * 128].astype(BF16)
        cb = _dot_nt(cg, bg.astype(BF16))
        hg = h_ref[s, :, g * 256:(g + 1) * 256]
        yoff = _dot(cg, hg.astype(BF16)) * eacs_e[:, g * 256:(g + 1) * 256]
        ydiag = []
        for hh in range(4):
            hd = g * 4 + hh
            seg = jnp.exp(jnp.where(causal, acs[:, hd:hd + 1] - acs_t[hd:hd + 1, :], NEG))
            ydiag.append(_dot((cb * seg).astype(BF16), xdt_b[:, hd * 64:(hd + 1) * 64]))
        ys.append(yoff + jnp.concatenate(ydiag, axis=1))
        st = _dot(bg.T.astype(BF16), xd[:, g * 256:(g + 1) * 256])
        h_ref[s, :, g * 256:(g + 1) * 256] = hg * cdec_e[:, g * 256:(g + 1) * 256] + st

    y = jnp.concatenate(ys, axis=1) + dskip_ref[...] * xs
    zz = z_ref[s]
    y = y * (zz * _sigmoid(zz))
    nw = nw_ref[...]
    for g in range(2):
        yg = y[:, g * 256:(g + 1) * 256]
        ms = jnp.mean(yg * yg, axis=-1, keepdims=True)
        o_ref[s, :, g * 256:(g + 1) * 256] = (
            yg * lax.rsqrt(ms + NORM_EPS) * nw[:, g * 256:(g + 1) * 256]).astype(o_ref.dtype)


def _ssd(z, xbc, dt, dtb, alog, dskip_e, nw, consts):
    b, s, _ = z.shape
    L = L_SSD
    nb = next(n for n in (4, 2, 1) if b % n == 0)
    fixed = lambda bi, i: (0, 0)
    tok = lambda bi, i: (bi, i, 0)
    return pl.pallas_call(
        _ssd_kernel,
        grid=(b // nb, s // L),
        in_specs=[pl.BlockSpec((nb, L, 512), tok),
                  pl.BlockSpec((nb, L, 1024), tok),
                  pl.BlockSpec((nb, L, 128), tok),
                  pl.BlockSpec((1, 128), fixed),
                  pl.BlockSpec((1, 128), fixed),
                  pl.BlockSpec((1, 512), fixed),
                  pl.BlockSpec((1, 512), fixed),
                  pl.BlockSpec((L, L), fixed),
                  pl.BlockSpec((128, 512), fixed)],
        out_specs=pl.BlockSpec((nb, L, 512), tok),
        out_shape=jax.ShapeDtypeStruct((b, s, 512), BF16),
        scratch_shapes=[pltpu.VMEM((nb, 128, 512), F32)],
        compiler_params=_cparams(("parallel", "arbitrary")),
        name="ssd",
    )(z, xbc, dt, dtb, alog, dskip_e, nw, consts["ssd_tri"], consts["ssd_ex"])


def _cmp_kernel(x_ref, pos_ref, w1_ref, w2_ref, o_ref):
    nsb = o_ref.shape[2]
    pos = pos_ref[0]
    lo = jnp.zeros((nsb, 256), F32)
    hi = jnp.zeros((nsb, 256), F32)
    for t in range(NSA_CMP_STRIDE):
        xt = x_ref[0, pl.ds(t, nsb, stride=NSA_CMP_STRIDE), :]
        lo = lo + _dot((xt + pos[t:t + 1, :]).astype(BF16), w1_ref[0, t])
        u = t + NSA_CMP_STRIDE
        hi = hi + _dot((xt + pos[u:u + 1, :]).astype(BF16), w1_ref[0, u])
    hid = lo + pltpu.roll(hi, nsb - 1, 0)
    gl = 0.5 * hid * (1.0 + jnp.tanh(0.7978845608028654 * (hid + 0.044715 * hid * hid * hid)))
    o_ref[0, 0] = _dot(gl.astype(BF16), w2_ref[0]).astype(o_ref.dtype)


def _compress(x, pos, w1, w2):
    b, s, _ = x.shape
    nsb = s // NSA_CMP_STRIDE
    return pl.pallas_call(
        _cmp_kernel,
        grid=(b, 2),
        in_specs=[pl.BlockSpec((1, s, 128), lambda bi, c: (bi, 0, c)),
                  pl.BlockSpec((1, NSA_CMP_LEN, 128), lambda bi, c: (c, 0, 0)),
                  pl.BlockSpec((1, NSA_CMP_LEN, 128, 256), lambda bi, c: (c, 0, 0, 0)),
                  pl.BlockSpec((1, 256, 128), lambda bi, c: (c, 0, 0))],
        out_specs=pl.BlockSpec((1, 1, nsb, 128), lambda bi, c: (c, bi, 0, 0)),
        out_shape=jax.ShapeDtypeStruct((2, b, nsb, 128), BF16),
        compiler_params=_cparams(("parallel", "parallel")),
        name="nsa_compress",
    )(x, pos, w1, w2)


def _block_diag2(w):
    z = jnp.zeros_like(w)
    return jnp.concatenate([jnp.concatenate([w, z], axis=-1),
                            jnp.concatenate([z, w], axis=-1)], axis=-2)


def _nsa_kernel(q_ref, kc_ref, vc_ref, ks_ref, vs_ref, kw_ref, vw_ref, gate_ref, nw_ref, ov_ref,
                o_ref):
    tq = q_ref.shape[1]
    nsb = kc_ref.shape[1]
    tk = TK_NSA
    t0 = pl.program_id(1) * tq
    groups = (0, 1)

    lane = lax.broadcasted_iota(jnp.int32, (tq, LANES), 1)
    lane_lo = lane < 64
    gmask = (lane_lo, lane >= 64)
    tpos = t0 + lax.broadcasted_iota(jnp.int32, (tq, 1), 0)

    q2, qs = [], []
    for g in groups:
        qa = q_ref[0, :, g * 128:(g + 1) * 128].astype(F32)
        qb = pltpu.roll(qa, 64, 1)
        h0, h1 = (qa, qb) if g == 0 else (qb, qa)
        q2.append(jnp.concatenate([jnp.where(gmask[g], h0, 0.0), jnp.where(gmask[g], h1, 0.0)],
                                  axis=0).astype(BF16))
        qs.append((jnp.where(lane_lo, qa, 0.0), jnp.where(lane_lo, qb, 0.0)))

    nidx = lax.broadcasted_iota(jnp.int32, (tq, nsb), 1)
    cvalid = (nidx * NSA_CMP_STRIDE + (NSA_CMP_LEN - 1)) <= tpos

    def select(ncw, nb):
        o_cmp, imp_ts = [], []
        for g in groups:
            s = _dot_nt(q2[g], kc_ref[0, 0:ncw, :]).reshape(2, tq, ncw)
            s = jnp.where(cvalid[None, :, 0:ncw], s, NEG)
            e = jnp.where(cvalid[None, :, 0:ncw],
                          jnp.exp2(s - jnp.max(s, axis=-1, keepdims=True)), 0.0)
            den = jnp.sum(e, axis=-1, keepdims=True)
            p = e / jnp.where(den > 0.0, den, 1.0)
            oc = _dot(p.reshape(2 * tq, ncw).astype(BF16), vc_ref[0, 0:ncw, :])
            if g == 1:
                oc = pltpu.roll(oc, 64, 1)
            o_cmp.append(oc.reshape(2, tq, LANES))
            imp = _split_dot_lhs(p[0] + p[1], ov_ref[0:ncw, :], 2)
            imp_ts.append(imp.T[0:nb, :])

        imp_t = jnp.concatenate(imp_ts, axis=1)
        blk = lax.broadcasted_iota(jnp.int32, (nb, 2 * tq), 0)
        tcol = lax.broadcasted_iota(jnp.int32, (1, 2 * tq), 1)
        qblk = (t0 + jnp.where(tcol >= tq, tcol - tq, tcol)) >> 6
        forced = (blk == 0) | ((blk <= qblk) & (blk > qblk - NSA_SEL_LOCAL))
        future = blk > qblk
        blk_f = blk.astype(F32)

        def pick(_, work):
            mx = jnp.max(work, axis=0, keepdims=True)
            first = jnp.min(jnp.where(work == mx, blk_f, 1e9), axis=0, keepdims=True)
            return jnp.where(blk_f == first, -3e38, work)

        n_forced = 1 + NSA_SEL_LOCAL
        work = lax.fori_loop(0, NSA_SEL_TOPK - n_forced, pick,
                             jnp.where(forced, -3e38, jnp.where(future, NEG, imp_t)))
        veto_t = jnp.where(work < -2e38, jnp.where(future, -1.0, 0.0), -1.0)
        if nb < LANES:
            veto_t = jnp.concatenate([veto_t, jnp.full((LANES - nb, 2 * tq), -1.0, F32)], axis=0)
        return (o_cmp[0], o_cmp[1], veto_t[:, 0:tq].T, veto_t[:, tq:2 * tq].T)

    nclass = max(1, min(4, nsb // LANES))
    cw, bw = nsb // nclass, LANES // nclass
    size_class = jnp.minimum((t0 + tq - NSA_CMP_LEN) // (cw * NSA_CMP_STRIDE), nclass - 1)
    oc0, oc1, veto0, veto1 = lax.switch(
        size_class, [functools.partial(select, cw * (i + 1), bw * (i + 1)) for i in range(nclass)])
    o_cmp, veto = (oc0, oc1), (veto0, veto1)

    flag_lane = (lane >= 64) & (lane < 64 + NSA_FLAGS)

    def slc_step(c, width, carry, diagonal=False):
        k0 = pl.multiple_of(c * tk, tk)
        shift = (64 - NSA_FLAGS * c) & 127
        if diagonal:
            causal = (lax.broadcasted_iota(jnp.int32, (tq, width), 1)
                      <= lax.broadcasted_iota(jnp.int32, (tq, width), 0) + (width - tq))
        out = []
        for g in groups:
            m, acc = carry[g]
            ks = ks_ref[g, 0, pl.ds(k0, width), :]
            vs = vs_ref[g, 0, pl.ds(k0, width), :]
            aug = jnp.where(flag_lane, pltpu.roll(veto[g], shift, 1), 0.0)
            qq = jnp.concatenate([qs[g][0] + aug, qs[g][1] + aug], axis=0).astype(BF16)
            sc = _dot_nt(qq, ks).reshape(2, tq, width)
            if diagonal:
                sc = jnp.where(causal[None], sc, NEG)
            m_new = jnp.maximum(m, jnp.max(sc, axis=-1, keepdims=True))
            pe = jnp.exp2(sc - m_new).reshape(2 * tq, width).astype(BF16)
            pv = _dot(pe, vs).reshape(2, tq, LANES)
            out.append((m_new, jnp.exp2(m - m_new) * acc + pv))
        return tuple(out)

    init = (jnp.full((2, tq, 1), NEG, F32), jnp.zeros((2, tq, LANES), F32))
    nfull = t0 // tk
    carry = lax.fori_loop(
        0, nfull // 2, lambda c, cr: slc_step(2 * c + 1, tk, slc_step(2 * c, tk, cr)),
        (init, init))
    carry = lax.cond(nfull % 2 == 1, lambda cr: slc_step(nfull - 1, tk, cr), lambda cr: cr, carry)
    carry = lax.switch(
        (t0 - nfull * tk) // tq,
        [functools.partial(slc_step, nfull, (j + 1) * tq, diagonal=True) for j in range(tk // tq)],
        carry)

    lw = tq + NSA_WINDOW
    w0 = pl.multiple_of(jnp.maximum(t0 - NSA_WINDOW, 0), tq)
    kw = kw_ref[0, pl.ds(w0, lw), :]
    dist = tpos - (w0 + lax.broadcasted_iota(jnp.int32, (tq, lw), 1))
    wvalid = jnp.where(dist >= 0, dist, NSA_WINDOW) < NSA_WINDOW

    sg = _sigmoid(gate_ref[0])
    for g in groups:
        acc_s = carry[g][1]
        o_slc = acc_s / acc_s[:, :, 64:65]
        sw = jnp.where(wvalid[None], _dot_nt(q2[g], kw).reshape(2, tq, lw), NEG)
        ew = jnp.exp2(sw - jnp.max(sw, axis=-1, keepdims=True))
        ow = _dot(ew.reshape(2 * tq, lw).astype(BF16), vw_ref[g, 0, pl.ds(w0, lw), :])
        ow = ow.reshape(2, tq, LANES)
        o_win = ow / ow[:, :, 64:65]
        outs = []
        for r in range(2):
            gcol = [jnp.sum(jnp.where(lane == g * 6 + r * 3 + c, sg, 0.0), axis=-1, keepdims=True)
                    for c in range(3)]
            o = gcol[0] * o_cmp[g][r] + gcol[1] * o_slc[r] + gcol[2] * o_win[r]
            ms = jnp.sum(jnp.where(lane_lo, o * o, 0.0), axis=-1, keepdims=True) * (1.0 / NSA_DH)
            outs.append(o * lax.rsqrt(ms + NORM_EPS))
        o_ref[0, :, g * 128:(g + 1) * 128] = (
            jnp.where(lane_lo, outs[0], pltpu.roll(outs[1], 64, 1))
            * nw_ref[:, g * 128:(g + 1) * 128]).astype(o_ref.dtype)


def _nsa(q, cmp, ks, vs, kw, vw, gate, nw, consts):
    b, s, _ = q.shape
    nsb = cmp.shape[2]
    tq = TQ_NSA
    return pl.pallas_call(
        _nsa_kernel,
        grid=(b, s // tq),
        in_specs=[pl.BlockSpec((1, tq, 256), lambda bi, i: (bi, i, 0)),
                  pl.BlockSpec((None, 1, nsb, 128), lambda bi, i: (0, bi, 0, 0)),
                  pl.BlockSpec((None, 1, nsb, 128), lambda bi, i: (1, bi, 0, 0)),
                  pl.BlockSpec((2, 1, s, 128), lambda bi, i: (0, bi, 0, 0)),
                  pl.BlockSpec((2, 1, s, 128), lambda bi, i: (0, bi, 0, 0)),
                  pl.BlockSpec((1, s, 128), lambda bi, i: (bi, 0, 0)),
                  pl.BlockSpec((2, 1, s, 128), lambda bi, i: (0, bi, 0, 0)),
                  pl.BlockSpec((1, tq, 128), lambda bi, i: (bi, i, 0)),
                  pl.BlockSpec((1, 256), lambda bi, i: (0, 0)),
                  pl.BlockSpec((nsb, 128), lambda bi, i: (0, 0))],
        out_specs=pl.BlockSpec((1, tq, 256), lambda bi, i: (bi, i, 0)),
        out_shape=jax.ShapeDtypeStruct((b, s, 256), BF16),
        compiler_params=_cparams(("parallel", "arbitrary")),
        name="nsa_attn",
    )(q, cmp, cmp, ks, vs, kw, vw, gate, nw, consts["nsa_overlap"])


def _outmlp_kernel(h_ref, yg_ref, yn_ref, ys_ref, wog_ref, won_ref, wos_ref, n2_ref,
                   wup_ref, wdn_ref, fn_ref, o_ref, acc_ref, u_ref, *, final_norm):
    k = pl.program_id(1)

    @pl.when(k == 0)
    def _():
        h2 = (h_ref[...] + _dot(yg_ref[...], wog_ref[...]) + _dot(yn_ref[...], won_ref[...])
              + _dot(ys_ref[...], wos_ref[...]))
        acc_ref[...] = h2
        u = h2 * lax.rsqrt(jnp.mean(h2 * h2, axis=-1, keepdims=True) + NORM_EPS) * n2_ref[...]
        u_ref[...] = u.astype(BF16)

    a = jnp.maximum(_dot(u_ref[...], wup_ref[...]), 0.0)
    acc_ref[...] += _dot((a * a).astype(BF16), wdn_ref[...])

    @pl.when(k == pl.num_programs(1) - 1)
    def _():
        out = acc_ref[...]
        if final_norm:
            out = (out * lax.rsqrt(jnp.mean(out * out, axis=-1, keepdims=True) + NORM_EPS)
                   * fn_ref[...])
        o_ref[...] = out


def _outmlp(h2d, yg, yn, ys, wo, n2, wup, wdn, fn, layer, final_norm):
    m = h2d.shape[0]
    tm, tf = TM_MLP, TF_MLP
    row = lambda i, k: (i, 0)
    fixed = lambda i, k: (0, 0)
    return pl.pallas_call(
        functools.partial(_outmlp_kernel, final_norm=final_norm),
        grid=(m // tm, D_FF // tf),
        in_specs=[pl.BlockSpec((tm, D_MODEL), row),
                  pl.BlockSpec((tm, 256), row),
                  pl.BlockSpec((tm, 256), row),
                  pl.BlockSpec((tm, 512), row),
                  pl.BlockSpec((None, 256, D_MODEL), lambda i, k: (layer, 0, 0)),
                  pl.BlockSpec((None, 256, D_MODEL), lambda i, k: (layer, 1, 0)),
                  pl.BlockSpec((None, 512, D_MODEL), lambda i, k: (layer, 1, 0)),
                  pl.BlockSpec((1, D_MODEL), fixed),
                  pl.BlockSpec((None, D_MODEL, tf), lambda i, k: (layer, 0, k)),
                  pl.BlockSpec((None, tf, D_MODEL), lambda i, k: (layer, k, 0)),
                  pl.BlockSpec((1, D_MODEL), fixed)],
        out_specs=pl.BlockSpec((tm, D_MODEL), row),
        out_shape=jax.ShapeDtypeStruct((m, D_MODEL), F32),
        scratch_shapes=[pltpu.VMEM((tm, D_MODEL), F32), pltpu.VMEM((tm, D_MODEL), BF16)],
        compiler_params=_cparams(("parallel", "arbitrary")),
        name="outproj_mlp",
    )(h2d, yg, yn, ys, wo, wo, wo, n2, wup, wdn, fn)


def _constants(seq):
    c = {}
    i = np.arange(TT_GLA)
    c["gla_tri"] = ((i[:, None] >= i[None, :]) & (i[:, None] // GLA_SUB == i[None, :] // GLA_SUB))
    dk = np.arange(128) // GLA_DK
    dv = np.arange(256) // GLA_DV
    c["gla_hm"] = dk[:, None] == dv[None, :]
    c["gla_hm2"] = dv[:, None] == dk[None, :]
    c["gla_hmean"] = (dv[:, None] == dv[None, :]) / float(GLA_DV)
    i = np.arange(L_SSD)
    c["ssd_tri"] = i[:, None] >= i[None, :]
    hd = np.arange(128)
    c["ssd_ex"] = hd[:, None] == (np.arange(512) // 64)[None, :]
    nsb = seq // NSA_CMP_STRIDE
    cs = np.arange(nsb) * NSA_CMP_STRIDE
    ss = np.arange(LANES) * NSA_SEL_BLOCK
    c["nsa_overlap"] = ((cs[:, None] < ss[None, :] + NSA_SEL_BLOCK)
                        & (cs[:, None] + NSA_CMP_LEN > ss[None, :]))
    out = {k: jnp.asarray(np.asarray(v, np.float32), BF16) for k, v in c.items()}
    out["gla_hm2"] = out["gla_hm2"].astype(F32)
    return out


def _rope_tables(seq):
    half = NSA_DH // 2
    inv = ROPE_THETA ** (-jnp.arange(half, dtype=F32) / half)
    ang = jnp.arange(seq).astype(F32)[:, None] * inv[None, :]
    cos, sin = jnp.cos(ang), jnp.sin(ang)
    return jnp.tile(cos, (1, 4)), jnp.tile(jnp.concatenate([-sin, sin], axis=1), (1, 2))


def _pad_cols(w, width):
    return jnp.pad(w, ((0, 0), (0, 0), (0, width - w.shape[2])))


def _proj_weights(w_in):
    gq, gk, gv, glr, gr, nq, nkv, ngate, sz, sxbc, sdt = jnp.split(
        w_in.astype(BF16), np.cumsum(IN_SPLITS)[:-1].tolist(), axis=2)
    return dict(
        a=jnp.concatenate([gq, gk, gv, gr, _pad_cols(glr, 128), _pad_cols(ngate, 128)], axis=2),
        b=jnp.concatenate([sz, sxbc, _pad_cols(sdt, 128), nq], axis=2),
        kvall=nkv)


def kernel(x, norm1_w, w_in, gla_gate_w2, gla_gate_b, gla_norm_w, nsa_cmp_pos_k, nsa_cmp_w1_k, nsa_cmp_w2_k, nsa_cmp_pos_v, nsa_cmp_w1_v, nsa_cmp_w2_v, nsa_norm_w, ssd_conv_w, ssd_conv_b, ssd_dt_bias, ssd_a_log, ssd_d, ssd_norm_w, w_out, norm2_w, w_up, w_down, final_norm_w):
    bsz, seq, _ = x.shape
    depth = w_in.shape[0]
    m = bsz * seq
    nsb = seq // NSA_CMP_STRIDE
    assert seq % 2048 == 0 and seq // NSA_SEL_BLOCK <= LANES
    consts = _constants(seq)
    cos, sin = _rope_tables(seq)
    h = x.reshape(m, D_MODEL)
    w_proj = _proj_weights(w_in)
    wo_b, wup_b, wdn_b = w_out.astype(BF16), w_up.astype(BF16), w_down.astype(BF16)
    for l in range(depth):
        p = _inproj(h, norm1_w[l][None, :], cos, sin, w_proj, ssd_conv_w[l],
                    ssd_conv_b[l][None, :], l, seq)

        y_gla = _gla(p["gla"].reshape(bsz, seq, 896),
                     jnp.pad(gla_gate_w2[l], ((0, 112), (0, 0))).astype(BF16),
                     gla_gate_b[l][None, :], gla_norm_w[l][None, :], consts)

        y_ssd = _ssd(p["z"].reshape(bsz, seq, 512), p["xbc"].reshape(bsz, seq, 1024),
                     p["dt"].reshape(bsz, seq, 128),
                     jnp.pad(ssd_dt_bias[l], (0, 120))[None, :],
                     jnp.pad(ssd_a_log[l], (0, 120))[None, :],
                     jnp.repeat(ssd_d[l], 64)[None, :], ssd_norm_w[l][None, :], consts)

        pos = jnp.tile(jnp.stack([nsa_cmp_pos_k[l], nsa_cmp_pos_v[l]]), (1, 1, 2))
        w1 = jnp.stack([nsa_cmp_w1_k[l], nsa_cmp_w1_v[l]]).astype(BF16)
        w1 = _block_diag2(w1.reshape(2, NSA_CMP_LEN, NSA_DH, 128))
        w2 = _block_diag2(jnp.stack([nsa_cmp_w2_k[l], nsa_cmp_w2_v[l]]).astype(BF16))
        cmp = _compress(p["cmp"].reshape(bsz, seq, 256), pos, w1, w2)

        y_nsa = _nsa(p["q"].reshape(bsz, seq, 256), cmp,
                     p["ks"].reshape(2, bsz, seq, 128), p["vs"].reshape(2, bsz, seq, 128),
                     p["kw"].reshape(bsz, seq, 128), p["vw"].reshape(2, bsz, seq, 128),
                     p["gate"].reshape(bsz, seq, 128),
                     nsa_norm_w[l][None, :], consts)

        h = _outmlp(h, y_gla.reshape(m, 256), y_nsa.reshape(m, 256), y_ssd.reshape(m, 512),
                    wo_b, norm2_w[l][None, :], wup_b, wdn_b, final_norm_w[None, :], l,
                    final_norm=(l == depth - 1))
    return h.reshape(bsz, seq, D_MODEL)
```

```python
import functools

import jax
import jax.numpy as jnp
import numpy as np
from jax import lax
from jax.experimental import pallas as pl
from jax.experimental.pallas import tpu as pltpu

F32 = jnp.float32
BF16 = jnp.bfloat16

D_MODEL = 1024
GLA_DK, GLA_DV = 32, 64
GLA_TAU = 16.0
GLA_SUB = 16
NSA_DH = 64
NSA_CMP_LEN, NSA_CMP_STRIDE = 32, 16
NSA_SEL_BLOCK, NSA_SEL_TOPK, NSA_SEL_LOCAL = 64, 16, 2
NSA_WINDOW = 512
SSD_D_INNER = 512
SSD_CONV = 4
SSD_CONV_DIM = 1024
D_FF = 4 * D_MODEL
ROPE_THETA = 10000.0
NORM_EPS = 1e-6
NEG = -1e30
BIG = 1e30
LANES = 128
VMEM_LIMIT = 56 * 1024 * 1024

IN_SPLITS = (128, 128, 256, 16, 256, 256, 768, 12, 512, SSD_CONV_DIM, 8)

TM_PROJ = 512
TT_GLA = 128
L_SSD = 128
TQ_NSA = 256
TK_NSA = 2048
NSA_FLAGS = TK_NSA // NSA_SEL_BLOCK
Q_SCALE = NSA_DH ** -0.5 * 1.4426950408889634
TM_MLP = 512


def _cparams(sem):
    return pltpu.CompilerParams(dimension_semantics=sem, vmem_limit_bytes=VMEM_LIMIT)


def _dot(a, b):
    return jnp.dot(a, b, preferred_element_type=F32)


def _dot_nt(a, b):
    return lax.dot_general(a, b, (((1,), (1,)), ((), ())), preferred_element_type=F32)


def _split_dot_lhs(a, b_bf16, terms):
    acc, rem = None, a
    for _ in range(terms):
        hi = rem.astype(BF16)
        part = _dot(hi, b_bf16)
        acc = part if acc is None else acc + part
        rem = rem - hi.astype(F32)
    return acc


def _split_dot_rhs(a_bf16, b, terms):
    acc, rem = None, b
    for _ in range(terms):
        hi = rem.astype(BF16)
        part = _dot(a_bf16, hi)
        acc = part if acc is None else acc + part
        rem = rem - hi.astype(F32)
    return acc


def _sigmoid(x):
    return 1.0 / (1.0 + jnp.exp(-x))


def _softplus(x):
    return jnp.maximum(x, 0.0) + jnp.log1p(jnp.exp(-jnp.abs(x)))


def _rope_apply(x, cos, sin_signed, lane_lo):
    w = x.shape[-1]
    partner = jnp.where(lane_lo, pltpu.roll(x, w - 32, 1), pltpu.roll(x, 32, 1))
    return x * cos + partner * sin_signed


def _inproj_kernel(x_ref, xprev_ref, nw_ref, cos_ref, sin_ref, wa_ref, wb_ref, wkv_ref,
                   cw_ref, cb_ref,
                   gla_ref, q_ref, cmp_ref, ks_ref, vs_ref, kw_ref, vw_ref, gate_ref, z_ref,
                   xbc_ref, dt_ref, xe_ref, *, tiles_per_seq):
    tm = x_ref.shape[0]
    half = tm // 2
    halo = xprev_ref.shape[0]
    seq_start = (pl.program_id(0) % tiles_per_seq) == 0
    cw = cw_ref[...]

    def normed(x):
        y = x * lax.rsqrt(jnp.mean(x * x, axis=-1, keepdims=True) + NORM_EPS) * nw_ref[...]
        return y.astype(BF16)

    for part, lo in enumerate((0, half)):
        rows = slice(lo, lo + half)
        u = normed(x_ref[rows, :])
        if part == 0:
            pb = _dot(jnp.concatenate([normed(xprev_ref[...]), u], axis=0), wb_ref[...])
            before = jnp.where(seq_start, 0.0, pb[0:halo, 512:1536])
            pb = pb[halo:, :]
        else:
            pb = _dot(u, wb_ref[...])
            before = raw_tail
        raw = pb[:, 512:1536]
        raw_tail = raw[half - halo:, :]
        z_ref[rows, :] = pb[:, 0:512]
        dt_ref[rows, :] = pb[:, 1536:1664]
        xe_ref[part, 0:halo, :] = before
        xe_ref[part, halo:halo + half, :] = raw
        conv = jnp.broadcast_to(cb_ref[...], (half, SSD_CONV_DIM))
        for w in range(SSD_CONV):
            conv = conv + xe_ref[part, pl.ds(halo - (SSD_CONV - 1) + w, half), :] * cw[w:w + 1, :]
        xbc_ref[rows, :] = conv * _sigmoid(conv)

        cos = cos_ref[rows, :]
        sin = sin_ref[rows, :]
        lane = lax.broadcasted_iota(jnp.int32, cos.shape, 1)
        lane_lo = (lane & 63) < 32
        q = pb[:, 1664:1920]
        for c in range(2):
            qc = _rope_apply(q[:, c * 128:(c + 1) * 128], cos, sin, lane_lo)
            q_ref[rows, c * 128:(c + 1) * 128] = (qc * Q_SCALE).astype(BF16)
        kv = _dot(u, wkv_ref[...])
        cmp_ref[rows, 0:128] = _rope_apply(kv[:, 0:128], cos, sin, lane_lo)
        cmp_ref[rows, 128:256] = kv[:, 128:256]
        ksl = _rope_apply(kv[:, 256:384], cos, sin, lane_lo)
        pos = ((pl.program_id(0) % tiles_per_seq) * tm + lo
               + lax.broadcasted_iota(jnp.int32, (half, 1), 0))
        flag = jnp.where((lane - 64) == ((pos >> 6) & (NSA_FLAGS - 1)), BIG, 0.0)
        ks_ref[0, rows, :] = jnp.where(lane < 64, ksl, flag).astype(BF16)
        ks_ref[1, rows, :] = jnp.where(lane < 64, pltpu.roll(ksl, 64, 1), flag).astype(BF16)
        kw_ref[rows, :] = _rope_apply(kv[:, 512:640], cos, sin, lane_lo).astype(BF16)
        one = jnp.where(lane == 64, 1.0, 0.0)
        for v_ref, vv in ((vs_ref, kv[:, 384:512]), (vw_ref, kv[:, 640:768])):
            v_ref[0, rows, :] = jnp.where(lane < 64, vv, one).astype(BF16)
            v_ref[1, rows, :] = jnp.where(lane < 64, pltpu.roll(vv, 64, 1), one).astype(BF16)
        pa = _dot(u, wa_ref[...])
        gla_ref[rows, :] = pa[:, 0:896]
        gate_ref[rows, :] = pa[:, 896:1024]


_PROJ_OUT = (("gla", 896, F32), ("q", 256, BF16), ("cmp", 256, F32), ("ks", 128, BF16),
             ("vs", 128, BF16), ("kw", 128, BF16), ("vw", 128, BF16), ("gate", 128, F32),
             ("z", 512, F32), ("xbc", 1024, F32), ("dt", 128, F32))
_PROJ_PER_GROUP = ("ks", "vs", "vw")
_PROJ_W = ("a", "b", "kvall")


def _inproj(h2d, nw, cos, sin, w, cw, cb, layer, seq):
    m = h2d.shape[0]
    tm = TM_PROJ
    nt = seq // tm
    halo = 16
    row = lambda i: (i, 0)
    fixed = lambda i: (0, 0)

    def ospec(name, wd):
        if name in _PROJ_PER_GROUP:
            return pl.BlockSpec((2, tm, wd), lambda i: (0, i, 0))
        return pl.BlockSpec((tm, wd), row)

    def oshape(name, wd, dt):
        return jax.ShapeDtypeStruct((2, m, wd) if name in _PROJ_PER_GROUP else (m, wd), dt)

    outs = pl.pallas_call(
        functools.partial(_inproj_kernel, tiles_per_seq=nt),
        grid=(m // tm,),
        in_specs=[pl.BlockSpec((tm, D_MODEL), row),
                  pl.BlockSpec((halo, D_MODEL),
                               lambda i: (jnp.maximum(i * (tm // halo) - 1, 0), 0)),
                  pl.BlockSpec((1, D_MODEL), fixed),
                  pl.BlockSpec((tm, LANES), lambda i: (i % nt, 0)),
                  pl.BlockSpec((tm, LANES), lambda i: (i % nt, 0))]
                 + [pl.BlockSpec((None,) + w[n].shape[1:], lambda i: (layer, 0, 0))
                    for n in _PROJ_W]
                 + [pl.BlockSpec((SSD_CONV, SSD_CONV_DIM), fixed),
                    pl.BlockSpec((1, SSD_CONV_DIM), fixed)],
        out_specs=[ospec(n, wd) for n, wd, _ in _PROJ_OUT],
        out_shape=[oshape(n, wd, dt) for n, wd, dt in _PROJ_OUT],
        scratch_shapes=[pltpu.VMEM((2, tm // 2 + halo, SSD_CONV_DIM), F32)],
        compiler_params=_cparams(("parallel",)),
        name="inproj",
    )(h2d, h2d, nw, cos, sin, *[w[n] for n in _PROJ_W], cw, cb)
    return {n: o for (n, _, _), o in zip(_PROJ_OUT, outs)}


def _gla_kernel(x_ref, w2_ref, bg_ref, nw_ref, tri_ref, hm_ref, hm2_ref, hmean_ref,
                o_ref, st_ref):
    @pl.when(pl.program_id(1) == 0)
    def _():
        st_ref[...] = jnp.zeros_like(st_ref)

    for s in range(x_ref.shape[0]):
        _gla_tile(s, x_ref, w2_ref, bg_ref, nw_ref, tri_ref, hm_ref, hm2_ref, hmean_ref,
                  o_ref, st_ref)


def _gla_tile(s, x_ref, w2_ref, bg_ref, nw_ref, tri_ref, hm_ref, hm2_ref, hmean_ref,
              o_ref, st_ref):
    tt = x_ref.shape[1]
    nsub = tt // GLA_SUB
    x = x_ref[s]
    q = x[:, 0:128] * (GLA_DK ** -0.5)
    k = x[:, 128:256]
    v = x[:, 256:512]
    r = x[:, 512:768]
    glr = x[:, 768:896].astype(BF16)
    pre = _dot(glr, w2_ref[...]) + bg_ref[...]
    log_a = -_softplus(-pre) * (1.0 / GLA_TAU)
    bc = _split_dot_rhs(tri_ref[...], log_a, 3)

    q3 = q.reshape(nsub, GLA_SUB, 128)
    k3 = k.reshape(nsub, GLA_SUB, 128)
    bc3 = bc.reshape(nsub, GLA_SUB, 128)
    v3 = v.reshape(nsub, GLA_SUB, 256)
    row = lax.broadcasted_iota(jnp.int32, (nsub, GLA_SUB, 128), 1)
    hm = hm_ref[...]

    o = jnp.zeros((tt, 256), F32)
    for j in range(GLA_SUB):
        diff = bc3 - bc3[:, j:j + 1, :]
        w = jnp.exp(jnp.where(row >= j, diff, NEG))
        t = (q3 * k3[:, j:j + 1, :] * w).reshape(tt, 128).astype(BF16)
        a = _dot(t, hm).reshape(nsub, GLA_SUB, 256)
        o = o + (a * v3[:, j:j + 1, :]).reshape(tt, 256)

    gtot3 = bc3[:, GLA_SUB - 1:GLA_SUB, :]
    kdec = (k3 * jnp.exp(gtot3 - bc3)).reshape(tt, 128).astype(BF16)
    qdec = q * jnp.exp(bc)
    bends = [bc[GLA_SUB - 1:GLA_SUB, :]]
    for c in range(1, nsub):
        bends.append(bends[-1] + bc[(c + 1) * GLA_SUB - 1:(c + 1) * GLA_SUB, :])
    bstart = jnp.concatenate(
        [jnp.zeros((GLA_SUB, 128), F32)]
        + [jnp.broadcast_to(bends[c], (GLA_SUB, 128)) for c in range(nsub - 1)], axis=0)
    st0 = st_ref[s]
    o = o + _dot_nt((qdec * jnp.exp(bstart)).astype(BF16), st0.astype(BF16))

    vt = v.T.astype(BF16)
    col = lax.broadcasted_iota(jnp.int32, (256, tt), 1)
    rowi = lax.broadcasted_iota(jnp.int32, (tt, 128), 0)
    hm2 = hm2_ref[...]
    st = st0 * jnp.exp(bends[nsub - 1])
    for b in range(nsub):
        lo = b * GLA_SUB
        vsel = jnp.where(col >= lo, jnp.where(col < lo + GLA_SUB, vt, 0.0), 0.0).astype(BF16)
        upd = _dot(vsel, kdec) * hm2
        if b < nsub - 1:
            later = jnp.exp(jnp.where(rowi >= lo + GLA_SUB, bstart - bends[b], NEG))
            o = o + _dot_nt((qdec * later).astype(BF16), upd.astype(BF16))
            st = st + upd * jnp.exp(bends[nsub - 1] - bends[b])
        else:
            st = st + upd
    st_ref[s] = st

    ms = _split_dot_lhs(o * o, hmean_ref[...], 2)
    y = o * lax.rsqrt(ms + NORM_EPS) * nw_ref[...]
    o_ref[s] = (y * (r * _sigmoid(r))).astype(o_ref.dtype)


def _gla(gla_in, w2, bg, nw, consts):
    b, s, _ = gla_in.shape
    tt = TT_GLA
    nb = next(n for n in (4, 2, 1) if b % n == 0)
    fixed = lambda bi, i: (0, 0)
    return pl.pallas_call(
        _gla_kernel,
        grid=(b // nb, s // tt),
        in_specs=[pl.BlockSpec((nb, tt, 896), lambda bi, i: (bi, i, 0)),
                  pl.BlockSpec((128, 128), fixed),
                  pl.BlockSpec((1, 128), fixed),
                  pl.BlockSpec((1, 256), fixed),
                  pl.BlockSpec((tt, tt), fixed),
                  pl.BlockSpec((128, 256), fixed),
                  pl.BlockSpec((256, 128), fixed),
                  pl.BlockSpec((256, 256), fixed)],
        out_specs=pl.BlockSpec((nb, tt, 256), lambda bi, i: (bi, i, 0)),
        out_shape=jax.ShapeDtypeStruct((b, s, 256), BF16),
        scratch_shapes=[pltpu.VMEM((nb, 256, 128), F32)],
        compiler_params=_cparams(("parallel", "arbitrary")),
        name="gla",
    )(gla_in, w2, bg, nw, consts["gla_tri"], consts["gla_hm"], consts["gla_hm2"],
      consts["gla_hmean"])


def _ssd_kernel(z_ref, x_ref, dt_ref, dtb_ref, alog_ref,
                dskip_ref, nw_ref, tri_ref, ex_ref, o_ref, h_ref):
    @pl.when(pl.program_id(1) == 0)
    def _():
        h_ref[...] = jnp.zeros_like(h_ref)

    for s in range(x_ref.shape[0]):
        _ssd_chunk(s, z_ref, x_ref, dt_ref, dtb_ref, alog_ref, dskip_ref, nw_ref, tri_ref, ex_ref,
                   o_ref, h_ref)


def _ssd_chunk(s, z_ref, x_ref, dt_ref, dtb_ref, alog_ref, dskip_ref, nw_ref, tri_ref, ex_ref,
               o_ref, h_ref):
    L = x_ref.shape[1]
    xbc = x_ref[s]
    xs = xbc[:, 0:512]
    bm = xbc[:, 512:768]
    cm = xbc[:, 768:1024]

    dt = _softplus(dt_ref[s] + dtb_ref[...])
    a = dt * (-jnp.exp(alog_ref[...]))
    acs = _split_dot_rhs(tri_ref[...], a, 3)
    ex = ex_ref[...]
    dt_e = _split_dot_lhs(dt, ex, 3)
    acs_e = _split_dot_lhs(acs, ex, 3)
    acs_last_e = acs_e[L - 1:L, :]
    xdt = xs * dt_e
    xdt_b = xdt.astype(BF16)
    xd = (xdt * jnp.exp(acs_last_e - acs_e)).astype(BF16)
    eacs_e = jnp.exp(acs_e)
    cdec_e = jnp.exp(acs_last_e)
    acs_t = acs.T
    ri = lax.broadcasted_iota(jnp.int32, (L, L), 0)
    ci = lax.broadcasted_iota(jnp.int32, (L, L), 1)
    causal = ri >= ci

    ys = []
    for g in range(2):
        bg = bm[:, g * 128:(g + 1) * 128]
        cg = cm[:, g * 128:(g + 1) * 128].astype(BF16)
        cb = _dot_nt(cg, bg.astype(BF16))
        hg = h_ref[s, :, g * 256:(g + 1) * 256]
        yoff = _dot(cg, hg.astype(BF16)) * eacs_e[:, g * 256:(g + 1) * 256]
        ydiag = []
        for hh in range(4):
            hd = g * 4 + hh
            seg = jnp.exp(jnp.where(causal, acs[:, hd:hd + 1] - acs_t[hd:hd + 1, :], NEG))
            ydiag.append(_dot((cb * seg).astype(BF16), xdt_b[:, hd * 64:(hd + 1) * 64]))
        ys.append(yoff + jnp.concatenate(ydiag, axis=1))
        st = _dot(bg.T.astype(BF16), xd[:, g * 256:(g + 1) * 256])
        h_ref[s, :, g * 256:(g + 1) * 256] = hg * cdec_e[:, g * 256:(g + 1) * 256] + st

    y = jnp.concatenate(ys, axis=1) + dskip_ref[...] * xs
    zz = z_ref[s]
    y = y * (zz * _sigmoid(zz))
    nw = nw_ref[...]
    for g in range(2):
        yg = y[:, g * 256:(g + 1) * 256]
        ms = jnp.mean(yg * yg, axis=-1, keepdims=True)
        o_ref[s, :, g * 256:(g + 1) * 256] = (
            yg * lax.rsqrt(ms + NORM_EPS) * nw[:, g * 256:(g + 1) * 256]).astype(o_ref.dtype)


def _ssd(z, xbc, dt, dtb, alog, dskip_e, nw, consts):
    b, s, _ = z.shape
    L = L_SSD
    nb = next(n for n in (4, 2, 1) if b % n == 0)
    fixed = lambda bi, i: (0, 0)
    tok = lambda bi, i: (bi, i, 0)
    return pl.pallas_call(
        _ssd_kernel,
        grid=(b // nb, s // L),
        in_specs=[pl.BlockSpec((nb, L, 512), tok),
                  pl.BlockSpec((nb, L, 1024), tok),
                  pl.BlockSpec((nb, L, 128), tok),
                  pl.BlockSpec((1, 128), fixed),
                  pl.BlockSpec((1, 128), fixed),
                  pl.BlockSpec((1, 512), fixed),
                  pl.BlockSpec((1, 512), fixed),
                  pl.BlockSpec((L, L), fixed),
                  pl.BlockSpec((128, 512), fixed)],
        out_specs=pl.BlockSpec((nb, L, 512), tok),
        out_shape=jax.ShapeDtypeStruct((b, s, 512), BF16),
        scratch_shapes=[pltpu.VMEM((nb, 128, 512), F32)],
        compiler_params=_cparams(("parallel", "arbitrary")),
        name="ssd",
    )(z, xbc, dt, dtb, alog, dskip_e, nw, consts["ssd_tri"], consts["ssd_ex"])


def _cmp_kernel(x_ref, pos_ref, w1_ref, w2_ref, o_ref):
    nsb = o_ref.shape[2]
    pos = pos_ref[0]
    lo = jnp.zeros((nsb, 256), F32)
    hi = jnp.zeros((nsb, 256), F32)
    for t in range(NSA_CMP_STRIDE):
        xt = x_ref[0, pl.ds(t, nsb, stride=NSA_CMP_STRIDE), :]
        lo = lo + _dot((xt + pos[t:t + 1, :]).astype(BF16), w1_ref[0, t])
        u = t + NSA_CMP_STRIDE
        hi = hi + _dot((xt + pos[u:u + 1, :]).astype(BF16), w1_ref[0, u])
    hid = lo + pltpu.roll(hi, nsb - 1, 0)
    gl = 0.5 * hid * (1.0 + jnp.tanh(0.7978845608028654 * (hid + 0.044715 * hid * hid * hid)))
    o_ref[0, 0] = _dot(gl.astype(BF16), w2_ref[0]).astype(o_ref.dtype)


def _compress(x, pos, w1, w2):
    b, s, _ = x.shape
    nsb = s // NSA_CMP_STRIDE
    return pl.pallas_call(
        _cmp_kernel,
        grid=(b, 2),
        in_specs=[pl.BlockSpec((1, s, 128), lambda bi, c: (bi, 0, c)),
                  pl.BlockSpec((1, NSA_CMP_LEN, 128), lambda bi, c: (c, 0, 0)),
                  pl.BlockSpec((1, NSA_CMP_LEN, 128, 256), lambda bi, c: (c, 0, 0, 0)),
                  pl.BlockSpec((1, 256, 128), lambda bi, c: (c, 0, 0))],
        out_specs=pl.BlockSpec((1, 1, nsb, 128), lambda bi, c: (c, bi, 0, 0)),
        out_shape=jax.ShapeDtypeStruct((2, b, nsb, 128), BF16),
        compiler_params=_cparams(("parallel", "parallel")),
        name="nsa_compress",
    )(x, pos, w1, w2)


def _block_diag2(w):
    z = jnp.zeros_like(w)
    return jnp.concatenate([jnp.concatenate([w, z], axis=-1),
                            jnp.concatenate([z, w], axis=-1)], axis=-2)


def _nsa_kernel(q_ref, kc_ref, vc_ref, ks_ref, vs_ref, kw_ref, vw_ref, gate_ref, nw_ref, ov_ref,
                o_ref):
    tq = q_ref.shape[1]
    nsb = kc_ref.shape[1]
    tk = TK_NSA
    t0 = pl.program_id(1) * tq
    groups = (0, 1)

    lane = lax.broadcasted_iota(jnp.int32, (tq, LANES), 1)
    lane_lo = lane < 64
    gmask = (lane_lo, lane >= 64)
    tpos = t0 + lax.broadcasted_iota(jnp.int32, (tq, 1), 0)

    q2, qs = [], []
    for g in groups:
        qa = q_ref[0, :, g * 128:(g + 1) * 128].astype(F32)
        qb = pltpu.roll(qa, 64, 1)
        h0, h1 = (qa, qb) if g == 0 else (qb, qa)
        q2.append(jnp.concatenate([jnp.where(gmask[g], h0, 0.0), jnp.where(gmask[g], h1, 0.0)],
                                  axis=0).astype(BF16))
        qs.append((jnp.where(lane_lo, qa, 0.0), jnp.where(lane_lo, qb, 0.0)))

    nidx = lax.broadcasted_iota(jnp.int32, (tq, nsb), 1)
    cvalid = (nidx * NSA_CMP_STRIDE + (NSA_CMP_LEN - 1)) <= tpos

    def select(ncw, nb):
        o_cmp, imp_ts = [], []
        for g in groups:
            s = _dot_nt(q2[g], kc_ref[0, 0:ncw, :]).reshape(2, tq, ncw)
            s = jnp.where(cvalid[None, :, 0:ncw], s, NEG)
            e = jnp.where(cvalid[None, :, 0:ncw],
                          jnp.exp2(s - jnp.max(s, axis=-1, keepdims=True)), 0.0)
            den = jnp.sum(e, axis=-1, keepdims=True)
            p = e / jnp.where(den > 0.0, den, 1.0)
            oc = _dot(p.reshape(2 * tq, ncw).astype(BF16), vc_ref[0, 0:ncw, :])
            if g == 1:
                oc = pltpu.roll(oc, 64, 1)
            o_cmp.append(oc.reshape(2, tq, LANES))
            imp = _split_dot_lhs(p[0] + p[1], ov_ref[0:ncw, :], 2)
            imp_ts.append(imp.T[0:nb, :])

        imp_t = jnp.concatenate(imp_ts, axis=1)
        blk = lax.broadcasted_iota(jnp.int32, (nb, 2 * tq), 0)
        tcol = lax.broadcasted_iota(jnp.int32, (1, 2 * tq), 1)
        qblk = (t0 + jnp.where(tcol >= tq, tcol - tq, tcol)) >> 6
        forced = (blk == 0) | ((blk <= qblk) & (blk > qblk - NSA_SEL_LOCAL))
        future = blk > qblk
        blk_f = blk.astype(F32)

        def pick(_, work):
            mx = jnp.max(work, axis=0, keepdims=True)
            first = jnp.min(jnp.where(work == mx, blk_f, 1e9), axis=0, keepdims=True)
            return jnp.where(blk_f == first, -3e38, work)

        n_forced = 1 + NSA_SEL_LOCAL
        work = lax.fori_loop(0, NSA_SEL_TOPK - n_forced, pick,
                             jnp.where(forced, -3e38, jnp.where(future, NEG, imp_t)))
        veto_t = jnp.where(work < -2e38, jnp.where(future, -1.0, 0.0), -1.0)
        if nb < LANES:
            veto_t = jnp.concatenate([veto_t, jnp.full((LANES - nb, 2 * tq), -1.0, F32)], axis=0)
        return (o_cmp[0], o_cmp[1], veto_t[:, 0:tq].T, veto_t[:, tq:2 * tq].T)

    nclass = max(1, min(4, nsb // LANES))
    cw, bw = nsb // nclass, LANES // nclass
    size_class = jnp.minimum((t0 + tq - NSA_CMP_LEN) // (cw * NSA_CMP_STRIDE), nclass - 1)
    oc0, oc1, veto0, veto1 = lax.switch(
        size_class, [functools.partial(select, cw * (i + 1), bw * (i + 1)) for i in range(nclass)])
    o_cmp, veto = (oc0, oc1), (veto0, veto1)

    flag_lane = (lane >= 64) & (lane < 64 + NSA_FLAGS)

    def slc_step(c, width, carry, diagonal=False):
        k0 = pl.multiple_of(c * tk, tk)
        shift = (64 - NSA_FLAGS * c) & 127
        if diagonal:
            causal = (lax.broadcasted_iota(jnp.int32, (tq, width), 1)
                      <= lax.broadcasted_iota(jnp.int32, (tq, width), 0) + (width - tq))
        out = []
        for g in groups:
            m, acc = carry[g]
            ks = ks_ref[g, 0, pl.ds(k0, width), :]
            vs = vs_ref[g, 0, pl.ds(k0, width), :]
            aug = jnp.where(flag_lane, pltpu.roll(veto[g], shift, 1), 0.0)
            qq = jnp.concatenate([qs[g][0] + aug, qs[g][1] + aug], axis=0).astype(BF16)
            sc = _dot_nt(qq, ks).reshape(2, tq, width)
            if diagonal:
                sc = jnp.where(causal[None], sc, NEG)
            m_new = jnp.maximum(m, jnp.max(sc, axis=-1, keepdims=True))
            pe = jnp.exp2(sc - m_new).reshape(2 * tq, width).astype(BF16)
            pv = _dot(pe, vs).reshape(2, tq, LANES)
            out.append((m_new, jnp.exp2(m - m_new) * acc + pv))
        return tuple(out)

    init = (jnp.full((2, tq, 1), NEG, F32), jnp.zeros((2, tq, LANES), F32))
    nfull = t0 // tk
    carry = lax.fori_loop(
        0, nfull // 2, lambda c, cr: slc_step(2 * c + 1, tk, slc_step(2 * c, tk, cr)),
        (init, init))
    carry = lax.cond(nfull % 2 == 1, lambda cr: slc_step(nfull - 1, tk, cr), lambda cr: cr, carry)
    carry = lax.switch(
        (t0 - nfull * tk) // tq,
        [functools.partial(slc_step, nfull, (j + 1) * tq, diagonal=True) for j in range(tk // tq)],
        carry)

    lw = tq + NSA_WINDOW
    w0 = pl.multiple_of(jnp.maximum(t0 - NSA_WINDOW, 0), tq)
    kw = kw_ref[0, pl.ds(w0, lw), :]
    dist = tpos - (w0 + lax.broadcasted_iota(jnp.int32, (tq, lw), 1))
    wvalid = jnp.where(dist >= 0, dist, NSA_WINDOW) < NSA_WINDOW

    sg = _sigmoid(gate_ref[0])
    for g in groups:
        acc_s = carry[g][1]
        o_slc = acc_s / acc_s[:, :, 64:65]
        sw = jnp.where(wvalid[None], _dot_nt(q2[g], kw).reshape(2, tq, lw), NEG)
        ew = jnp.exp2(sw - jnp.max(sw, axis=-1, keepdims=True))
        ow = _dot(ew.reshape(2 * tq, lw).astype(BF16), vw_ref[g, 0, pl.ds(w0, lw), :])
        ow = ow.reshape(2, tq, LANES)
        o_win = ow / ow[:, :, 64:65]
        outs = []
        for r in range(2):
            gcol = [jnp.sum(jnp.where(lane == g * 6 + r * 3 + c, sg, 0.0), axis=-1, keepdims=True)
                    for c in range(3)]
            o = gcol[0] * o_cmp[g][r] + gcol[1] * o_slc[r] + gcol[2] * o_win[r]
            ms = jnp.sum(jnp.where(lane_lo, o * o, 0.0), axis=-1, keepdims=True) * (1.0 / NSA_DH)
            outs.append(o * lax.rsqrt(ms + NORM_EPS))
        o_ref[0, :, g * 128:(g + 1) * 128] = (
            jnp.where(lane_lo, outs[0], pltpu.roll(outs[1], 64, 1))
            * nw_ref[:, g * 128:(g + 1) * 128]).astype(o_ref.dtype)


def _nsa(q, cmp, ks, vs, kw, vw, gate, nw, consts):
    b, s, _ = q.shape
    nsb = cmp.shape[2]
    tq = TQ_NSA
    return pl.pallas_call(
        _nsa_kernel,
        grid=(b, s // tq),
        in_specs=[pl.BlockSpec((1, tq, 256), lambda bi, i: (bi, i, 0)),
                  pl.BlockSpec((None, 1, nsb, 128), lambda bi, i: (0, bi, 0, 0)),
                  pl.BlockSpec((None, 1, nsb, 128), lambda bi, i: (1, bi, 0, 0)),
                  pl.BlockSpec((2, 1, s, 128), lambda bi, i: (0, bi, 0, 0)),
                  pl.BlockSpec((2, 1, s, 128), lambda bi, i: (0, bi, 0, 0)),
                  pl.BlockSpec((1, s, 128), lambda bi, i: (bi, 0, 0)),
                  pl.BlockSpec((2, 1, s, 128), lambda bi, i: (0, bi, 0, 0)),
                  pl.BlockSpec((1, tq, 128), lambda bi, i: (bi, i, 0)),
                  pl.BlockSpec((1, 256), lambda bi, i: (0, 0)),
                  pl.BlockSpec((nsb, 128), lambda bi, i: (0, 0))],
        out_specs=pl.BlockSpec((1, tq, 256), lambda bi, i: (bi, i, 0)),
        out_shape=jax.ShapeDtypeStruct((b, s, 256), BF16),
        compiler_params=_cparams(("parallel", "arbitrary")),
        name="nsa_attn",
    )(q, cmp, cmp, ks, vs, kw, vw, gate, nw, consts["nsa_overlap"])


def _outmlp_kernel(h_ref, yg_ref, yn_ref, ys_ref, wog_ref, won_ref, wos_ref, n2_ref,
                   wup_ref, wdn_ref, fn_ref, o_ref, *, final_norm):
    h2 = (h_ref[...] + _dot(yg_ref[...], wog_ref[...]) + _dot(yn_ref[...], won_ref[...])
          + _dot(ys_ref[...], wos_ref[...]))
    u = h2 * lax.rsqrt(jnp.mean(h2 * h2, axis=-1, keepdims=True) + NORM_EPS) * n2_ref[...]
    a = jnp.maximum(_dot(u.astype(BF16), wup_ref[...]), 0.0)
    out = h2 + _dot((a * a).astype(BF16), wdn_ref[...])
    if final_norm:
        out = (out * lax.rsqrt(jnp.mean(out * out, axis=-1, keepdims=True) + NORM_EPS)
               * fn_ref[...])
    o_ref[...] = out


def _outmlp(h2d, yg, yn, ys, wo, n2, wup, wdn, fn, layer, final_norm):
    m = h2d.shape[0]
    tm = TM_MLP
    row = lambda i: (i, 0)
    fixed = lambda i: (0, 0)
    whole = lambda i: (layer, 0, 0)
    return pl.pallas_call(
        functools.partial(_outmlp_kernel, final_norm=final_norm),
        grid=(m // tm,),
        in_specs=[pl.BlockSpec((tm, D_MODEL), row),
                  pl.BlockSpec((tm, 256), row),
                  pl.BlockSpec((tm, 256), row),
                  pl.BlockSpec((tm, 512), row),
                  pl.BlockSpec((None, 256, D_MODEL), lambda i: (layer, 0, 0)),
                  pl.BlockSpec((None, 256, D_MODEL), lambda i: (layer, 1, 0)),
                  pl.BlockSpec((None, 512, D_MODEL), lambda i: (layer, 1, 0)),
                  pl.BlockSpec((1, D_MODEL), fixed),
                  pl.BlockSpec((None, D_MODEL, D_FF), whole),
                  pl.BlockSpec((None, D_FF, D_MODEL), whole),
                  pl.BlockSpec((1, D_MODEL), fixed)],
        out_specs=pl.BlockSpec((tm, D_MODEL), row),
        out_shape=jax.ShapeDtypeStruct((m, D_MODEL), F32),
        compiler_params=_cparams(("parallel",)),
        name="outproj_mlp",
    )(h2d, yg, yn, ys, wo, wo, wo, n2, wup, wdn, fn)


def _constants(seq):
    c = {}
    i = np.arange(TT_GLA)
    c["gla_tri"] = ((i[:, None] >= i[None, :]) & (i[:, None] // GLA_SUB == i[None, :] // GLA_SUB))
    dk = np.arange(128) // GLA_DK
    dv = np.arange(256) // GLA_DV
    c["gla_hm"] = dk[:, None] == dv[None, :]
    c["gla_hm2"] = dv[:, None] == dk[None, :]
    c["gla_hmean"] = (dv[:, None] == dv[None, :]) / float(GLA_DV)
    i = np.arange(L_SSD)
    c["ssd_tri"] = i[:, None] >= i[None, :]
    hd = np.arange(128)
    c["ssd_ex"] = hd[:, None] == (np.arange(512) // 64)[None, :]
    nsb = seq // NSA_CMP_STRIDE
    cs = np.arange(nsb) * NSA_CMP_STRIDE
    ss = np.arange(LANES) * NSA_SEL_BLOCK
    c["nsa_overlap"] = ((cs[:, None] < ss[None, :] + NSA_SEL_BLOCK)
                        & (cs[:, None] + NSA_CMP_LEN > ss[None, :]))
    out = {k: jnp.asarray(np.asarray(v, np.float32), BF16) for k, v in c.items()}
    out["gla_hm2"] = out["gla_hm2"].astype(F32)
    return out


def _rope_tables(seq):
    half = NSA_DH // 2
    inv = ROPE_THETA ** (-jnp.arange(half, dtype=F32) / half)
    ang = jnp.arange(seq).astype(F32)[:, None] * inv[None, :]
    cos, sin = jnp.cos(ang), jnp.sin(ang)
    return jnp.tile(cos, (1, 4)), jnp.tile(jnp.concatenate([-sin, sin], axis=1), (1, 2))


def _pad_cols(w, width):
    return jnp.pad(w, ((0, 0), (0, 0), (0, width - w.shape[2])))


def _proj_weights(w_in):
    gq, gk, gv, glr, gr, nq, nkv, ngate, sz, sxbc, sdt = jnp.split(
        w_in.astype(BF16), np.cumsum(IN_SPLITS)[:-1].tolist(), axis=2)
    return dict(
        a=jnp.concatenate([gq, gk, gv, gr, _pad_cols(glr, 128), _pad_cols(ngate, 128)], axis=2),
        b=jnp.concatenate([sz, sxbc, _pad_cols(sdt, 128), nq], axis=2),
        kvall=nkv)


def kernel(x, norm1_w, w_in, gla_gate_w2, gla_gate_b, gla_norm_w, nsa_cmp_pos_k, nsa_cmp_w1_k, nsa_cmp_w2_k, nsa_cmp_pos_v, nsa_cmp_w1_v, nsa_cmp_w2_v, nsa_norm_w, ssd_conv_w, ssd_conv_b, ssd_dt_bias, ssd_a_log, ssd_d, ssd_norm_w, w_out, norm2_w, w_up, w_down, final_norm_w):
    bsz, seq, _ = x.shape
    depth = w_in.shape[0]
    m = bsz * seq
    nsb = seq // NSA_CMP_STRIDE
    assert seq % 2048 == 0 and seq // NSA_SEL_BLOCK <= LANES
    consts = _constants(seq)
    cos, sin = _rope_tables(seq)
    h = x.reshape(m, D_MODEL)
    w_proj = _proj_weights(w_in)
    wo_b, wup_b, wdn_b = w_out.astype(BF16), w_up.astype(BF16), w_down.astype(BF16)
    for l in range(depth):
        p = _inproj(h, norm1_w[l][None, :], cos, sin, w_proj, ssd_conv_w[l],
                    ssd_conv_b[l][None, :], l, seq)

        y_gla = _gla(p["gla"].reshape(bsz, seq, 896),
                     jnp.pad(gla_gate_w2[l], ((0, 112), (0, 0))).astype(BF16),
                     gla_gate_b[l][None, :], gla_norm_w[l][None, :], consts)

        y_ssd = _ssd(p["z"].reshape(bsz, seq, 512), p["xbc"].reshape(bsz, seq, 1024),
                     p["dt"].reshape(bsz, seq, 128),
                     jnp.pad(ssd_dt_bias[l], (0, 120))[None, :],
                     jnp.pad(ssd_a_log[l], (0, 120))[None, :],
                     jnp.repeat(ssd_d[l], 64)[None, :], ssd_norm_w[l][None, :], consts)

        pos = jnp.tile(jnp.stack([nsa_cmp_pos_k[l], nsa_cmp_pos_v[l]]), (1, 1, 2))
        w1 = jnp.stack([nsa_cmp_w1_k[l], nsa_cmp_w1_v[l]]).astype(BF16)
        w1 = _block_diag2(w1.reshape(2, NSA_CMP_LEN, NSA_DH, 128))
        w2 = _block_diag2(jnp.stack([nsa_cmp_w2_k[l], nsa_cmp_w2_v[l]]).astype(BF16))
        cmp = _compress(p["cmp"].reshape(bsz, seq, 256), pos, w1, w2)

        y_nsa = _nsa(p["q"].reshape(bsz, seq, 256), cmp,
                     p["ks"].reshape(2, bsz, seq, 128), p["vs"].reshape(2, bsz, seq, 128),
                     p["kw"].reshape(bsz, seq, 128), p["vw"].reshape(2, bsz, seq, 128),
                     p["gate"].reshape(bsz, seq, 128),
                     nsa_norm_w[l][None, :], consts)

        h = _outmlp(h, y_gla.reshape(m, 256), y_nsa.reshape(m, 256), y_ssd.reshape(m, 512),
                    wo_b, norm2_w[l][None, :], wup_b, wdn_b, final_norm_w[None, :], l,
                    final_norm=(l == depth - 1))
    return h.reshape(bsz, seq, D_MODEL)
```

```python
import functools

import jax
import jax.numpy as jnp
import numpy as np
from jax import lax
from jax.experimental import pallas as pl
from jax.experimental.pallas import tpu as pltpu

F32 = jnp.float32
BF16 = jnp.bfloat16

D_MODEL = 1024
GLA_DK, GLA_DV = 32, 64
GLA_TAU = 16.0
GLA_SUB = 16
NSA_DH = 64
NSA_CMP_LEN, NSA_CMP_STRIDE = 32, 16
NSA_SEL_BLOCK, NSA_SEL_TOPK, NSA_SEL_LOCAL = 64, 16, 2
NSA_WINDOW = 512
SSD_D_INNER = 512
SSD_CONV = 4
SSD_CONV_DIM = 1024
D_FF = 4 * D_MODEL
ROPE_THETA = 10000.0
NORM_EPS = 1e-6
NEG = -1e30
BIG = 1e30
LANES = 128
VMEM_LIMIT = 56 * 1024 * 1024

IN_SPLITS = (128, 128, 256, 16, 256, 256, 768, 12, 512, SSD_CONV_DIM, 8)

TM_PROJ = 512
PROJ_PARTS = 2
TT_GLA = 128
L_SSD = 128
TQ_NSA = 256
TK_NSA = 2048
NSA_FLAGS = TK_NSA // NSA_SEL_BLOCK
Q_SCALE = NSA_DH ** -0.5 * 1.4426950408889634
TM_MLP = 512


def _cparams(sem):
    return pltpu.CompilerParams(dimension_semantics=sem, vmem_limit_bytes=VMEM_LIMIT)


def _dot(a, b):
    return jnp.dot(a, b, preferred_element_type=F32)


def _dot_nt(a, b):
    return lax.dot_general(a, b, (((1,), (1,)), ((), ())), preferred_element_type=F32)


def _split_dot_lhs(a, b_bf16, terms):
    acc, rem = None, a
    for _ in range(terms):
        hi = rem.astype(BF16)
        part = _dot(hi, b_bf16)
        acc = part if acc is None else acc + part
        rem = rem - hi.astype(F32)
    return acc


def _split_dot_rhs(a_bf16, b, terms):
    acc, rem = None, b
    for _ in range(terms):
        hi = rem.astype(BF16)
        part = _dot(a_bf16, hi)
        acc = part if acc is None else acc + part
        rem = rem - hi.astype(F32)
    return acc


def _sigmoid(x):
    return 1.0 / (1.0 + jnp.exp(-x))


def _softplus(x):
    return jnp.maximum(x, 0.0) + jnp.log1p(jnp.exp(-jnp.abs(x)))


def _rope_apply(x, cos, sin_signed, lane_lo):
    w = x.shape[-1]
    partner = jnp.where(lane_lo, pltpu.roll(x, w - 32, 1), pltpu.roll(x, 32, 1))
    return x * cos + partner * sin_signed


def _inproj_kernel(x_ref, xprev_ref, nw_ref, cos_ref, sin_ref, wa_ref, wb_ref, wkv_ref,
                   cw_ref, cb_ref,
                   gla_ref, q_ref, cmp_ref, ks_ref, vs_ref, kw_ref, vw_ref, gate_ref, z_ref,
                   xbc_ref, dt_ref, xe_ref, *, tiles_per_seq):
    tm = x_ref.shape[0]
    nparts = xe_ref.shape[0]
    half = tm // nparts
    halo = xprev_ref.shape[0]
    seq_start = (pl.program_id(0) % tiles_per_seq) == 0
    cw = cw_ref[...]

    def normed(x):
        y = x * lax.rsqrt(jnp.mean(x * x, axis=-1, keepdims=True) + NORM_EPS) * nw_ref[...]
        return y.astype(BF16)

    for part in range(nparts):
        lo = part * half
        rows = slice(lo, lo + half)
        u = normed(x_ref[rows, :])
        if part == 0:
            pb = _dot(jnp.concatenate([normed(xprev_ref[...]), u], axis=0), wb_ref[...])
            before = jnp.where(seq_start, 0.0, pb[0:halo, 512:1536])
            pb = pb[halo:, :]
        else:
            pb = _dot(u, wb_ref[...])
            before = raw_tail
        raw = pb[:, 512:1536]
        raw_tail = raw[half - halo:, :]
        z_ref[rows, :] = pb[:, 0:512]
        dt_ref[rows, :] = pb[:, 1536:1664]
        xe_ref[part, 0:halo, :] = before
        xe_ref[part, halo:halo + half, :] = raw
        conv = jnp.broadcast_to(cb_ref[...], (half, SSD_CONV_DIM))
        for w in range(SSD_CONV):
            conv = conv + xe_ref[part, pl.ds(halo - (SSD_CONV - 1) + w, half), :] * cw[w:w + 1, :]
        xbc_ref[rows, :] = conv * _sigmoid(conv)

        cos = cos_ref[rows, :]
        sin = sin_ref[rows, :]
        lane = lax.broadcasted_iota(jnp.int32, cos.shape, 1)
        lane_lo = (lane & 63) < 32
        q = pb[:, 1664:1920]
        for c in range(2):
            qc = _rope_apply(q[:, c * 128:(c + 1) * 128], cos, sin, lane_lo)
            q_ref[rows, c * 128:(c + 1) * 128] = (qc * Q_SCALE).astype(BF16)
        kv = _dot(u, wkv_ref[...])
        cmp_ref[rows, 0:128] = _rope_apply(kv[:, 0:128], cos, sin, lane_lo)
        cmp_ref[rows, 128:256] = kv[:, 128:256]
        ksl = _rope_apply(kv[:, 256:384], cos, sin, lane_lo)
        pos = ((pl.program_id(0) % tiles_per_seq) * tm + lo
               + lax.broadcasted_iota(jnp.int32, (half, 1), 0))
        flag = jnp.where((lane - 64) == ((pos >> 6) & (NSA_FLAGS - 1)), BIG, 0.0)
        ks_ref[0, rows, :] = jnp.where(lane < 64, ksl, flag).astype(BF16)
        ks_ref[1, rows, :] = jnp.where(lane < 64, pltpu.roll(ksl, 64, 1), flag).astype(BF16)
        kw_ref[rows, :] = _rope_apply(kv[:, 512:640], cos, sin, lane_lo).astype(BF16)
        one = jnp.where(lane == 64, 1.0, 0.0)
        for v_ref, vv in ((vs_ref, kv[:, 384:512]), (vw_ref, kv[:, 640:768])):
            v_ref[0, rows, :] = jnp.where(lane < 64, vv, one).astype(BF16)
            v_ref[1, rows, :] = jnp.where(lane < 64, pltpu.roll(vv, 64, 1), one).astype(BF16)
        pa = _dot(u, wa_ref[...])
        gla_ref[rows, :] = pa[:, 0:896]
        gate_ref[rows, :] = pa[:, 896:1024]


_PROJ_OUT = (("gla", 896, F32), ("q", 256, BF16), ("cmp", 256, F32), ("ks", 128, BF16),
             ("vs", 128, BF16), ("kw", 128, BF16), ("vw", 128, BF16), ("gate", 128, F32),
             ("z", 512, F32), ("xbc", 1024, F32), ("dt", 128, F32))
_PROJ_PER_GROUP = ("ks", "vs", "vw")
_PROJ_W = ("a", "b", "kvall")


def _inproj(h2d, nw, cos, sin, w, cw, cb, layer, seq):
    m = h2d.shape[0]
    tm = TM_PROJ
    nt = seq // tm
    halo = 16
    row = lambda i: (i, 0)
    fixed = lambda i: (0, 0)

    def ospec(name, wd):
        if name in _PROJ_PER_GROUP:
            return pl.BlockSpec((2, tm, wd), lambda i: (0, i, 0))
        return pl.BlockSpec((tm, wd), row)

    def oshape(name, wd, dt):
        return jax.ShapeDtypeStruct((2, m, wd) if name in _PROJ_PER_GROUP else (m, wd), dt)

    outs = pl.pallas_call(
        functools.partial(_inproj_kernel, tiles_per_seq=nt),
        grid=(m // tm,),
        in_specs=[pl.BlockSpec((tm, D_MODEL), row),
                  pl.BlockSpec((halo, D_MODEL),
                               lambda i: (jnp.maximum(i * (tm // halo) - 1, 0), 0)),
                  pl.BlockSpec((1, D_MODEL), fixed),
                  pl.BlockSpec((tm, LANES), lambda i: (i % nt, 0)),
                  pl.BlockSpec((tm, LANES), lambda i: (i % nt, 0))]
                 + [pl.BlockSpec((None,) + w[n].shape[1:], lambda i: (layer, 0, 0))
                    for n in _PROJ_W]
                 + [pl.BlockSpec((SSD_CONV, SSD_CONV_DIM), fixed),
                    pl.BlockSpec((1, SSD_CONV_DIM), fixed)],
        out_specs=[ospec(n, wd) for n, wd, _ in _PROJ_OUT],
        out_shape=[oshape(n, wd, dt) for n, wd, dt in _PROJ_OUT],
        scratch_shapes=[pltpu.VMEM((PROJ_PARTS, tm // PROJ_PARTS + halo, SSD_CONV_DIM), F32)],
        compiler_params=_cparams(("parallel",)),
        name="inproj",
    )(h2d, h2d, nw, cos, sin, *[w[n] for n in _PROJ_W], cw, cb)
    return {n: o for (n, _, _), o in zip(_PROJ_OUT, outs)}


def _gla_kernel(x_ref, w2_ref, bg_ref, nw_ref, tri_ref, hm_ref, hm2_ref, hmean_ref,
                o_ref, st_ref):
    @pl.when(pl.program_id(1) == 0)
    def _():
        st_ref[...] = jnp.zeros_like(st_ref)

    for s in range(x_ref.shape[0]):
        _gla_tile(s, x_ref, w2_ref, bg_ref, nw_ref, tri_ref, hm_ref, hm2_ref, hmean_ref,
                  o_ref, st_ref)


def _gla_tile(s, x_ref, w2_ref, bg_ref, nw_ref, tri_ref, hm_ref, hm2_ref, hmean_ref,
              o_ref, st_ref):
    tt = x_ref.shape[1]
    nsub = tt // GLA_SUB
    x = x_ref[s]
    q = x[:, 0:128] * (GLA_DK ** -0.5)
    k = x[:, 128:256]
    v = x[:, 256:512]
    r = x[:, 512:768]
    glr = x[:, 768:896].astype(BF16)
    pre = _dot(glr, w2_ref[...]) + bg_ref[...]
    log_a = -_softplus(-pre) * (1.0 / GLA_TAU)
    bc = _split_dot_rhs(tri_ref[...], log_a, 3)

    q3 = q.reshape(nsub, GLA_SUB, 128)
    k3 = k.reshape(nsub, GLA_SUB, 128)
    bc3 = bc.reshape(nsub, GLA_SUB, 128)
    v3 = v.reshape(nsub, GLA_SUB, 256)
    row = lax.broadcasted_iota(jnp.int32, (nsub, GLA_SUB, 128), 1)
    hm = hm_ref[...]

    o = jnp.zeros((tt, 256), F32)
    for j in range(GLA_SUB):
        diff = bc3 - bc3[:, j:j + 1, :]
        w = jnp.exp(jnp.where(row >= j, diff, NEG))
        t = (q3 * k3[:, j:j + 1, :] * w).reshape(tt, 128).astype(BF16)
        a = _dot(t, hm).reshape(nsub, GLA_SUB, 256)
        o = o + (a * v3[:, j:j + 1, :]).reshape(tt, 256)

    gtot3 = bc3[:, GLA_SUB - 1:GLA_SUB, :]
    kdec = (k3 * jnp.exp(gtot3 - bc3)).reshape(tt, 128).astype(BF16)
    qdec = q * jnp.exp(bc)
    bends = [bc[GLA_SUB - 1:GLA_SUB, :]]
    for c in range(1, nsub):
        bends.append(bends[-1] + bc[(c + 1) * GLA_SUB - 1:(c + 1) * GLA_SUB, :])
    bstart = jnp.concatenate(
        [jnp.zeros((GLA_SUB, 128), F32)]
        + [jnp.broadcast_to(bends[c], (GLA_SUB, 128)) for c in range(nsub - 1)], axis=0)
    st0 = st_ref[s]
    o = o + _dot_nt((qdec * jnp.exp(bstart)).astype(BF16), st0.astype(BF16))

    vt = v.T.astype(BF16)
    col = lax.broadcasted_iota(jnp.int32, (256, tt), 1)
    rowi = lax.broadcasted_iota(jnp.int32, (tt, 128), 0)
    hm2 = hm2_ref[...]
    st = st0 * jnp.exp(bends[nsub - 1])
    for b in range(nsub):
        lo = b * GLA_SUB
        vsel = jnp.where(col >= lo, jnp.where(col < lo + GLA_SUB, vt, 0.0), 0.0).astype(BF16)
        upd = _dot(vsel, kdec) * hm2
        if b < nsub - 1:
            later = jnp.exp(jnp.where(rowi >= lo + GLA_SUB, bstart - bends[b], NEG))
            o = o + _dot_nt((qdec * later).astype(BF16), upd.astype(BF16))
            st = st + upd * jnp.exp(bends[nsub - 1] - bends[b])
        else:
            st = st + upd
    st_ref[s] = st

    ms = _split_dot_lhs(o * o, hmean_ref[...], 2)
    y = o * lax.rsqrt(ms + NORM_EPS) * nw_ref[...]
    o_ref[s] = (y * (r * _sigmoid(r))).astype(o_ref.dtype)


def _gla(gla_in, w2, bg, nw, consts):
    b, s, _ = gla_in.shape
    tt = TT_GLA
    nb = next(n for n in (4, 2, 1) if b % n == 0)
    fixed = lambda bi, i: (0, 0)
    return pl.pallas_call(
        _gla_kernel,
        grid=(b // nb, s // tt),
        in_specs=[pl.BlockSpec((nb, tt, 896), lambda bi, i: (bi, i, 0)),
                  pl.BlockSpec((128, 128), fixed),
                  pl.BlockSpec((1, 128), fixed),
                  pl.BlockSpec((1, 256), fixed),
                  pl.BlockSpec((tt, tt), fixed),
                  pl.BlockSpec((128, 256), fixed),
                  pl.BlockSpec((256, 128), fixed),
                  pl.BlockSpec((256, 256), fixed)],
        out_specs=pl.BlockSpec((nb, tt, 256), lambda bi, i: (bi, i, 0)),
        out_shape=jax.ShapeDtypeStruct((b, s, 256), BF16),
        scratch_shapes=[pltpu.VMEM((nb, 256, 128), F32)],
        compiler_params=_cparams(("parallel", "arbitrary")),
        name="gla",
    )(gla_in, w2, bg, nw, consts["gla_tri"], consts["gla_hm"], consts["gla_hm2"],
      consts["gla_hmean"])


def _ssd_kernel(z_ref, x_ref, dt_ref, dtb_ref, alog_ref,
                dskip_ref, nw_ref, tri_ref, ex_ref, o_ref, h_ref):
    @pl.when(pl.program_id(1) == 0)
    def _():
        h_ref[...] = jnp.zeros_like(h_ref)

    for s in range(x_ref.shape[0]):
        _ssd_chunk(s, z_ref, x_ref, dt_ref, dtb_ref, alog_ref, dskip_ref, nw_ref, tri_ref, ex_ref,
                   o_ref, h_ref)


def _ssd_chunk(s, z_ref, x_ref, dt_ref, dtb_ref, alog_ref, dskip_ref, nw_ref, tri_ref, ex_ref,
               o_ref, h_ref):
    L = x_ref.shape[1]
    xbc = x_ref[s]
    xs = xbc[:, 0:512]
    bm = xbc[:, 512:768]
    cm = xbc[:, 768:1024]

    dt = _softplus(dt_ref[s] + dtb_ref[...])
    a = dt * (-jnp.exp(alog_ref[...]))
    acs = _split_dot_rhs(tri_ref[...], a, 3)
    ex = ex_ref[...]
    dt_e = _split_dot_lhs(dt, ex, 3)
    acs_e = _split_dot_lhs(acs, ex, 3)
    acs_last_e = acs_e[L - 1:L, :]
    xdt = xs * dt_e
    xdt_b = xdt.astype(BF16)
    xd = (xdt * jnp.exp(acs_last_e - acs_e)).astype(BF16)
    eacs_e = jnp.exp(acs_e)
    cdec_e = jnp.exp(acs_last_e)
    acs_t = acs.T
    ri = lax.broadcasted_iota(jnp.int32, (L, L), 0)
    ci = lax.broadcasted_iota(jnp.int32, (L, L), 1)
    causal = ri >= ci

    ys = []
    for g in range(2):
        bg = bm[:, g * 128:(g + 1) * 128]
        cg = cm[:, g * 128:(g + 1) * 128].astype(BF16)
        cb = _dot_nt(cg, bg.astype(BF16))
        hg = h_ref[s, :, g * 256:(g + 1) * 256]
        yoff = _dot(cg, hg.astype(BF16)) * eacs_e[:, g * 256:(g + 1) * 256]
        ydiag = []
        for hh in range(4):
            hd = g * 4 + hh
            seg = jnp.exp(jnp.where(causal, acs[:, hd:hd + 1] - acs_t[hd:hd + 1, :], NEG))
            ydiag.append(_dot((cb * seg).astype(BF16), xdt_b[:, hd * 64:(hd + 1) * 64]))
        ys.append(yoff + jnp.concatenate(ydiag, axis=1))
        st = _dot(bg.T.astype(BF16), xd[:, g * 256:(g + 1) * 256])
        h_ref[s, :, g * 256:(g + 1) * 256] = hg * cdec_e[:, g * 256:(g + 1) * 256] + st

    y = jnp.concatenate(ys, axis=1) + dskip_ref[...] * xs
    zz = z_ref[s]
    y = y * (zz * _sigmoid(zz))
    nw = nw_ref[...]
    for g in range(2):
        yg = y[:, g * 256:(g + 1) * 256]
        ms = jnp.mean(yg * yg, axis=-1, keepdims=True)
        o_ref[s, :, g * 256:(g + 1) * 256] = (
            yg * lax.rsqrt(ms + NORM_EPS) * nw[:, g * 256:(g + 1) * 256]).astype(o_ref.dtype)


def _ssd(z, xbc, dt, dtb, alog, dskip_e, nw, consts):
    b, s, _ = z.shape
    L = L_SSD
    nb = next(n for n in (4, 2, 1) if b % n == 0)
    fixed = lambda bi, i: (0, 0)
    tok = lambda bi, i: (bi, i, 0)
    return pl.pallas_call(
        _ssd_kernel,
        grid=(b // nb, s // L),
        in_specs=[pl.BlockSpec((nb, L, 512), tok),
                  pl.BlockSpec((nb, L, 1024), tok),
                  pl.BlockSpec((nb, L, 128), tok),
                  pl.BlockSpec((1, 128), fixed),
                  pl.BlockSpec((1, 128), fixed),
                  pl.BlockSpec((1, 512), fixed),
                  pl.BlockSpec((1, 512), fixed),
                  pl.BlockSpec((L, L), fixed),
                  pl.BlockSpec((128, 512), fixed)],
        out_specs=pl.BlockSpec((nb, L, 512), tok),
        out_shape=jax.ShapeDtypeStruct((b, s, 512), BF16),
        scratch_shapes=[pltpu.VMEM((nb, 128, 512), F32)],
        compiler_params=_cparams(("parallel", "arbitrary")),
        name="ssd",
    )(z, xbc, dt, dtb, alog, dskip_e, nw, consts["ssd_tri"], consts["ssd_ex"])


def _cmp_kernel(x_ref, pos_ref, w1_ref, w2_ref, o_ref):
    nsb = o_ref.shape[2]
    pos = pos_ref[0]
    lo = jnp.zeros((nsb, 256), F32)
    hi = jnp.zeros((nsb, 256), F32)
    for t in range(NSA_CMP_STRIDE):
        xt = x_ref[0, pl.ds(t, nsb, stride=NSA_CMP_STRIDE), :]
        lo = lo + _dot((xt + pos[t:t + 1, :]).astype(BF16), w1_ref[0, t])
        u = t + NSA_CMP_STRIDE
        hi = hi + _dot((xt + pos[u:u + 1, :]).astype(BF16), w1_ref[0, u])
    hid = lo + pltpu.roll(hi, nsb - 1, 0)
    gl = 0.5 * hid * (1.0 + jnp.tanh(0.7978845608028654 * (hid + 0.044715 * hid * hid * hid)))
    o_ref[0, 0] = _dot(gl.astype(BF16), w2_ref[0]).astype(o_ref.dtype)


def _compress(x, pos, w1, w2):
    b, s, _ = x.shape
    nsb = s // NSA_CMP_STRIDE
    return pl.pallas_call(
        _cmp_kernel,
        grid=(b, 2),
        in_specs=[pl.BlockSpec((1, s, 128), lambda bi, c: (bi, 0, c)),
                  pl.BlockSpec((1, NSA_CMP_LEN, 128), lambda bi, c: (c, 0, 0)),
                  pl.BlockSpec((1, NSA_CMP_LEN, 128, 256), lambda bi, c: (c, 0, 0, 0)),
                  pl.BlockSpec((1, 256, 128), lambda bi, c: (c, 0, 0))],
        out_specs=pl.BlockSpec((1, 1, nsb, 128), lambda bi, c: (c, bi, 0, 0)),
        out_shape=jax.ShapeDtypeStruct((2, b, nsb, 128), BF16),
        compiler_params=_cparams(("parallel", "parallel")),
        name="nsa_compress",
    )(x, pos, w1, w2)


def _block_diag2(w):
    z = jnp.zeros_like(w)
    return jnp.concatenate([jnp.concatenate([w, z], axis=-1),
                            jnp.concatenate([z, w], axis=-1)], axis=-2)


def _nsa_kernel(q_ref, kc_ref, vc_ref, ks_ref, vs_ref, kw_ref, vw_ref, gate_ref, nw_ref, ov_ref,
                o_ref):
    tq = q_ref.shape[1]
    nsb = kc_ref.shape[1]
    tk = TK_NSA
    t0 = pl.program_id(1) * tq
    groups = (0, 1)

    lane = lax.broadcasted_iota(jnp.int32, (tq, LANES), 1)
    lane_lo = lane < 64
    gmask = (lane_lo, lane >= 64)
    tpos = t0 + lax.broadcasted_iota(jnp.int32, (tq, 1), 0)

    q2, qs = [], []
    for g in groups:
        qa = q_ref[0, :, g * 128:(g + 1) * 128].astype(F32)
        qb = pltpu.roll(qa, 64, 1)
        h0, h1 = (qa, qb) if g == 0 else (qb, qa)
        q2.append(jnp.concatenate([jnp.where(gmask[g], h0, 0.0), jnp.where(gmask[g], h1, 0.0)],
                                  axis=0).astype(BF16))
        qs.append((jnp.where(lane_lo, qa, 0.0), jnp.where(lane_lo, qb, 0.0)))

    nidx = lax.broadcasted_iota(jnp.int32, (tq, nsb), 1)
    cvalid = (nidx * NSA_CMP_STRIDE + (NSA_CMP_LEN - 1)) <= tpos
    lw = tq + NSA_WINDOW
    w0 = pl.multiple_of(jnp.maximum(t0 - NSA_WINDOW, 0), tq)
    dist = tpos - (w0 + lax.broadcasted_iota(jnp.int32, (tq, lw), 1))
    wvalid = jnp.where(dist >= 0, dist, NSA_WINDOW) < NSA_WINDOW

    def select(ncw, nb):
        o_cmp, imp_ts, o_win = [], [], []

        def window():
            kw = kw_ref[0, pl.ds(w0, lw), :]
            for g in groups:
                sw = jnp.where(wvalid[None], _dot_nt(q2[g], kw).reshape(2, tq, lw), NEG)
                ew = jnp.exp2(sw - jnp.max(sw, axis=-1, keepdims=True))
                ow = _dot(ew.reshape(2 * tq, lw).astype(BF16), vw_ref[g, 0, pl.ds(w0, lw), :])
                ow = ow.reshape(2, tq, LANES)
                o_win.append(ow / ow[:, :, 64:65])

        def compressed():
            for g in groups:
                s = _dot_nt(q2[g], kc_ref[0, 0:ncw, :]).reshape(2, tq, ncw)
                s = jnp.where(cvalid[None, :, 0:ncw], s, NEG)
                e = jnp.where(cvalid[None, :, 0:ncw],
                              jnp.exp2(s - jnp.max(s, axis=-1, keepdims=True)), 0.0)
                den = jnp.sum(e, axis=-1, keepdims=True)
                p = e / jnp.where(den > 0.0, den, 1.0)
                oc = _dot(p.reshape(2 * tq, ncw).astype(BF16), vc_ref[0, 0:ncw, :])
                if g == 1:
                    oc = pltpu.roll(oc, 64, 1)
                o_cmp.append(oc.reshape(2, tq, LANES))
                imp = _split_dot_lhs(p[0] + p[1], ov_ref[0:ncw, :], 2)
                imp_ts.append(imp.T[0:nb, :])

        for part in ((window, compressed) if ncw > LANES else (compressed, window)):
            part()

        imp_t = jnp.concatenate(imp_ts, axis=1)
        blk = lax.broadcasted_iota(jnp.int32, (nb, 2 * tq), 0)
        tcol = lax.broadcasted_iota(jnp.int32, (1, 2 * tq), 1)
        qblk = (t0 + jnp.where(tcol >= tq, tcol - tq, tcol)) >> 6
        forced = (blk == 0) | ((blk <= qblk) & (blk > qblk - NSA_SEL_LOCAL))
        future = blk > qblk
        blk_f = blk.astype(F32)

        def pick(_, work):
            mx = jnp.max(work, axis=0, keepdims=True)
            first = jnp.min(jnp.where(work == mx, blk_f, 1e9), axis=0, keepdims=True)
            return jnp.where(blk_f == first, -3e38, work)

        n_forced = 1 + NSA_SEL_LOCAL
        work = lax.fori_loop(0, NSA_SEL_TOPK - n_forced, pick,
                             jnp.where(forced, -3e38, jnp.where(future, NEG, imp_t)))
        veto_t = jnp.where(work < -2e38, jnp.where(future, -1.0, 0.0), -1.0)
        if nb < LANES:
            veto_t = jnp.concatenate([veto_t, jnp.full((LANES - nb, 2 * tq), -1.0, F32)], axis=0)
        return (o_cmp[0], o_cmp[1], veto_t[:, 0:tq].T, veto_t[:, tq:2 * tq].T,
                o_win[0], o_win[1])

    nclass = max(1, min(4, nsb // LANES))
    cw, bw = nsb // nclass, LANES // nclass
    size_class = jnp.minimum((t0 + tq - NSA_CMP_LEN) // (cw * NSA_CMP_STRIDE), nclass - 1)
    oc0, oc1, veto0, veto1, ow0, ow1 = lax.switch(
        size_class, [functools.partial(select, cw * (i + 1), bw * (i + 1)) for i in range(nclass)])
    o_cmp, veto, o_win = (oc0, oc1), (veto0, veto1), (ow0, ow1)

    flag_lane = (lane >= 64) & (lane < 64 + NSA_FLAGS)

    def slc_step(c, width, carry, diagonal=False):
        k0 = pl.multiple_of(c * tk, tk)
        shift = (64 - NSA_FLAGS * c) & 127
        if diagonal:
            causal = (lax.broadcasted_iota(jnp.int32, (tq, width), 1)
                      <= lax.broadcasted_iota(jnp.int32, (tq, width), 0) + (width - tq))
        out = []
        for g in groups:
            m, acc = carry[g]
            ks = ks_ref[g, 0, pl.ds(k0, width), :]
            vs = vs_ref[g, 0, pl.ds(k0, width), :]
            aug = jnp.where(flag_lane, pltpu.roll(veto[g], shift, 1), 0.0)
            qq = jnp.concatenate([qs[g][0] + aug, qs[g][1] + aug], axis=0).astype(BF16)
            sc = _dot_nt(qq, ks).reshape(2, tq, width)
            if diagonal:
                sc = jnp.where(causal[None], sc, NEG)
            m_new = jnp.maximum(m, jnp.max(sc, axis=-1, keepdims=True))
            pe = jnp.exp2(sc - m_new).reshape(2 * tq, width).astype(BF16)
            pv = _dot(pe, vs).reshape(2, tq, LANES)
            out.append((m_new, jnp.exp2(m - m_new) * acc + pv))
        return tuple(out)

    init = (jnp.full((2, tq, 1), NEG, F32), jnp.zeros((2, tq, LANES), F32))
    nfull = t0 // tk
    carry = lax.fori_loop(
        0, nfull // 2, lambda c, cr: slc_step(2 * c + 1, tk, slc_step(2 * c, tk, cr)),
        (init, init))
    carry = lax.cond(nfull % 2 == 1, lambda cr: slc_step(nfull - 1, tk, cr), lambda cr: cr, carry)
    carry = lax.switch(
        (t0 - nfull * tk) // tq,
        [functools.partial(slc_step, nfull, (j + 1) * tq, diagonal=True) for j in range(tk // tq)],
        carry)

    sg = _sigmoid(gate_ref[0])
    for g in groups:
        acc_s = carry[g][1]
        o_slc = acc_s / acc_s[:, :, 64:65]
        outs = []
        for r in range(2):
            gcol = [jnp.sum(jnp.where(lane == g * 6 + r * 3 + c, sg, 0.0), axis=-1, keepdims=True)
                    for c in range(3)]
            o = gcol[0] * o_cmp[g][r] + gcol[1] * o_slc[r] + gcol[2] * o_win[g][r]
            ms = jnp.sum(jnp.where(lane_lo, o * o, 0.0), axis=-1, keepdims=True) * (1.0 / NSA_DH)
            outs.append(o * lax.rsqrt(ms + NORM_EPS))
        o_ref[0, :, g * 128:(g + 1) * 128] = (
            jnp.where(lane_lo, outs[0], pltpu.roll(outs[1], 64, 1))
            * nw_ref[:, g * 128:(g + 1) * 128]).astype(o_ref.dtype)


def _nsa(q, cmp, ks, vs, kw, vw, gate, nw, consts):
    b, s, _ = q.shape
    nsb = cmp.shape[2]
    tq = TQ_NSA
    return pl.pallas_call(
        _nsa_kernel,
        grid=(b, s // tq),
        in_specs=[pl.BlockSpec((1, tq, 256), lambda bi, i: (bi, i, 0)),
                  pl.BlockSpec((None, 1, nsb, 128), lambda bi, i: (0, bi, 0, 0)),
                  pl.BlockSpec((None, 1, nsb, 128), lambda bi, i: (1, bi, 0, 0)),
                  pl.BlockSpec((2, 1, s, 128), lambda bi, i: (0, bi, 0, 0)),
                  pl.BlockSpec((2, 1, s, 128), lambda bi, i: (0, bi, 0, 0)),
                  pl.BlockSpec((1, s, 128), lambda bi, i: (bi, 0, 0)),
                  pl.BlockSpec((2, 1, s, 128), lambda bi, i: (0, bi, 0, 0)),
                  pl.BlockSpec((1, tq, 128), lambda bi, i: (bi, i, 0)),
                  pl.BlockSpec((1, 256), lambda bi, i: (0, 0)),
                  pl.BlockSpec((nsb, 128), lambda bi, i: (0, 0))],
        out_specs=pl.BlockSpec((1, tq, 256), lambda bi, i: (bi, i, 0)),
        out_shape=jax.ShapeDtypeStruct((b, s, 256), BF16),
        compiler_params=_cparams(("parallel", "arbitrary")),
        name="nsa_attn",
    )(q, cmp, cmp, ks, vs, kw, vw, gate, nw, consts["nsa_overlap"])


def _outmlp_kernel(h_ref, yg_ref, yn_ref, ys_ref, wog_ref, won_ref, wos_ref, n2_ref,
                   wup_ref, wdn_ref, fn_ref, o_ref, *, final_norm):
    h2 = (h_ref[...] + _dot(yg_ref[...], wog_ref[...]) + _dot(yn_ref[...], won_ref[...])
          + _dot(ys_ref[...], wos_ref[...]))
    u = h2 * lax.rsqrt(jnp.mean(h2 * h2, axis=-1, keepdims=True) + NORM_EPS) * n2_ref[...]
    a = jnp.maximum(_dot(u.astype(BF16), wup_ref[...]), 0.0)
    out = h2 + _dot((a * a).astype(BF16), wdn_ref[...])
    if final_norm:
        out = (out * lax.rsqrt(jnp.mean(out * out, axis=-1, keepdims=True) + NORM_EPS)
               * fn_ref[...])
    o_ref[...] = out


def _outmlp(h2d, yg, yn, ys, wo, n2, wup, wdn, fn, layer, final_norm):
    m = h2d.shape[0]
    tm = TM_MLP
    row = lambda i: (i, 0)
    fixed = lambda i: (0, 0)
    whole = lambda i: (layer, 0, 0)
    return pl.pallas_call(
        functools.partial(_outmlp_kernel, final_norm=final_norm),
        grid=(m // tm,),
        in_specs=[pl.BlockSpec((tm, D_MODEL), row),
                  pl.BlockSpec((tm, 256), row),
                  pl.BlockSpec((tm, 256), row),
                  pl.BlockSpec((tm, 512), row),
                  pl.BlockSpec((None, 256, D_MODEL), lambda i: (layer, 0, 0)),
                  pl.BlockSpec((None, 256, D_MODEL), lambda i: (layer, 1, 0)),
                  pl.BlockSpec((None, 512, D_MODEL), lambda i: (layer, 1, 0)),
                  pl.BlockSpec((1, D_MODEL), fixed),
                  pl.BlockSpec((None, D_MODEL, D_FF), whole),
                  pl.BlockSpec((None, D_FF, D_MODEL), whole),
                  pl.BlockSpec((1, D_MODEL), fixed)],
        out_specs=pl.BlockSpec((tm, D_MODEL), row),
        out_shape=jax.ShapeDtypeStruct((m, D_MODEL), F32),
        compiler_params=_cparams(("parallel",)),
        name="outproj_mlp",
    )(h2d, yg, yn, ys, wo, wo, wo, n2, wup, wdn, fn)


def _constants(seq):
    c = {}
    i = np.arange(TT_GLA)
    c["gla_tri"] = ((i[:, None] >= i[None, :]) & (i[:, None] // GLA_SUB == i[None, :] // GLA_SUB))
    dk = np.arange(128) // GLA_DK
    dv = np.arange(256) // GLA_DV
    c["gla_hm"] = dk[:, None] == dv[None, :]
    c["gla_hm2"] = dv[:, None] == dk[None, :]
    c["gla_hmean"] = (dv[:, None] == dv[None, :]) / float(GLA_DV)
    i = np.arange(L_SSD)
    c["ssd_tri"] = i[:, None] >= i[None, :]
    hd = np.arange(128)
    c["ssd_ex"] = hd[:, None] == (np.arange(512) // 64)[None, :]
    nsb = seq // NSA_CMP_STRIDE
    cs = np.arange(nsb) * NSA_CMP_STRIDE
    ss = np.arange(LANES) * NSA_SEL_BLOCK
    c["nsa_overlap"] = ((cs[:, None] < ss[None, :] + NSA_SEL_BLOCK)
                        & (cs[:, None] + NSA_CMP_LEN > ss[None, :]))
    out = {k: jnp.asarray(np.asarray(v, np.float32), BF16) for k, v in c.items()}
    out["gla_hm2"] = out["gla_hm2"].astype(F32)
    return out


def _rope_tables(seq):
    half = NSA_DH // 2
    inv = ROPE_THETA ** (-jnp.arange(half, dtype=F32) / half)
    ang = jnp.arange(seq).astype(F32)[:, None] * inv[None, :]
    cos, sin = jnp.cos(ang), jnp.sin(ang)
    return jnp.tile(cos, (1, 4)), jnp.tile(jnp.concatenate([-sin, sin], axis=1), (1, 2))


def _pad_cols(w, width):
    return jnp.pad(w, ((0, 0), (0, 0), (0, width - w.shape[2])))


def _proj_weights(w_in):
    gq, gk, gv, glr, gr, nq, nkv, ngate, sz, sxbc, sdt = jnp.split(
        w_in.astype(BF16), np.cumsum(IN_SPLITS)[:-1].tolist(), axis=2)
    return dict(
        a=jnp.concatenate([gq, gk, gv, gr, _pad_cols(glr, 128), _pad_cols(ngate, 128)], axis=2),
        b=jnp.concatenate([sz, sxbc, _pad_cols(sdt, 128), nq], axis=2),
        kvall=nkv)


def kernel(x, norm1_w, w_in, gla_gate_w2, gla_gate_b, gla_norm_w, nsa_cmp_pos_k, nsa_cmp_w1_k, nsa_cmp_w2_k, nsa_cmp_pos_v, nsa_cmp_w1_v, nsa_cmp_w2_v, nsa_norm_w, ssd_conv_w, ssd_conv_b, ssd_dt_bias, ssd_a_log, ssd_d, ssd_norm_w, w_out, norm2_w, w_up, w_down, final_norm_w):
    bsz, seq, _ = x.shape
    depth = w_in.shape[0]
    m = bsz * seq
    nsb = seq // NSA_CMP_STRIDE
    assert seq % 2048 == 0 and seq // NSA_SEL_BLOCK <= LANES
    consts = _constants(seq)
    cos, sin = _rope_tables(seq)
    h = x.reshape(m, D_MODEL)
    w_proj = _proj_weights(w_in)
    wo_b, wup_b, wdn_b = w_out.astype(BF16), w_up.astype(BF16), w_down.astype(BF16)
    for l in range(depth):
        p = _inproj(h, norm1_w[l][None, :], cos, sin, w_proj, ssd_conv_w[l],
                    ssd_conv_b[l][None, :], l, seq)

        y_gla = _gla(p["gla"].reshape(bsz, seq, 896),
                     jnp.pad(gla_gate_w2[l], ((0, 112), (0, 0))).astype(BF16),
                     gla_gate_b[l][None, :], gla_norm_w[l][None, :], consts)

        y_ssd = _ssd(p["z"].reshape(bsz, seq, 512), p["xbc"].reshape(bsz, seq, 1024),
                     p["dt"].reshape(bsz, seq, 128),
                     jnp.pad(ssd_dt_bias[l], (0, 120))[None, :],
                     jnp.pad(ssd_a_log[l], (0, 120))[None, :],
                     jnp.repeat(ssd_d[l], 64)[None, :], ssd_norm_w[l][None, :], consts)

        pos = jnp.tile(jnp.stack([nsa_cmp_pos_k[l], nsa_cmp_pos_v[l]]), (1, 1, 2))
        w1 = jnp.stack([nsa_cmp_w1_k[l], nsa_cmp_w1_v[l]]).astype(BF16)
        w1 = _block_diag2(w1.reshape(2, NSA_CMP_LEN, NSA_DH, 128))
        w2 = _block_diag2(jnp.stack([nsa_cmp_w2_k[l], nsa_cmp_w2_v[l]]).astype(BF16))
        cmp = _compress(p["cmp"].reshape(bsz, seq, 256), pos, w1, w2)

        y_nsa = _nsa(p["q"].reshape(bsz, seq, 256), cmp,
                     p["ks"].reshape(2, bsz, seq, 128), p["vs"].reshape(2, bsz, seq, 128),
                     p["kw"].reshape(bsz, seq, 128), p["vw"].reshape(2, bsz, seq, 128),
                     p["gate"].reshape(bsz, seq, 128),
                     nsa_norm_w[l][None, :], consts)

        h = _outmlp(h, y_gla.reshape(m, 256), y_nsa.reshape(m, 256), y_ssd.reshape(m, 512),
                    wo_b, norm2_w[l][None, :], wup_b, wdn_b, final_norm_w[None, :], l,
                    final_norm=(l == depth - 1))
    return h.reshape(bsz, seq, D_MODEL)
```

```python
import functools

import jax
import jax.numpy as jnp
import numpy as np
from jax import lax
from jax.experimental import pallas as pl
from jax.experimental.pallas import tpu as pltpu

F32 = jnp.float32
BF16 = jnp.bfloat16

D_MODEL = 1024
GLA_DK, GLA_DV = 32, 64
GLA_TAU = 16.0
GLA_SUB = 16
NSA_DH = 64
NSA_CMP_LEN, NSA_CMP_STRIDE = 32, 16
NSA_SEL_BLOCK, NSA_SEL_TOPK, NSA_SEL_LOCAL = 64, 16, 2
NSA_WINDOW = 512
SSD_D_INNER = 512
SSD_CONV = 4
SSD_CONV_DIM = 1024
D_FF = 4 * D_MODEL
ROPE_THETA = 10000.0
NORM_EPS = 1e-6
NEG = -1e30
BIG = 1e30
LANES = 128
VMEM_LIMIT = 56 * 1024 * 1024

IN_SPLITS = (128, 128, 256, 16, 256, 256, 768, 12, 512, SSD_CONV_DIM, 8)

TM_PROJ = 512
TT_GLA = 128
L_SSD = 256
TQ_NSA = 256
TK_NSA = 2048
NSA_FLAGS = TK_NSA // NSA_SEL_BLOCK
Q_SCALE = NSA_DH ** -0.5 * 1.4426950408889634
TM_MLP = 512


def _cparams(sem):
    return pltpu.CompilerParams(dimension_semantics=sem, vmem_limit_bytes=VMEM_LIMIT)


def _dot(a, b):
    return jnp.dot(a, b, preferred_element_type=F32)


def _dot_nt(a, b):
    return lax.dot_general(a, b, (((1,), (1,)), ((), ())), preferred_element_type=F32)


def _split_dot_lhs(a, b_bf16, terms):
    acc, rem = None, a
    for _ in range(terms):
        hi = rem.astype(BF16)
        part = _dot(hi, b_bf16)
        acc = part if acc is None else acc + part
        rem = rem - hi.astype(F32)
    return acc


def _split_dot_rhs(a_bf16, b, terms):
    acc, rem = None, b
    for _ in range(terms):
        hi = rem.astype(BF16)
        part = _dot(a_bf16, hi)
        acc = part if acc is None else acc + part
        rem = rem - hi.astype(F32)
    return acc


def _sigmoid(x):
    return 1.0 / (1.0 + jnp.exp(-x))


def _softplus(x):
    return jnp.maximum(x, 0.0) + jnp.log1p(jnp.exp(-jnp.abs(x)))


def _rope_apply(x, cos, sin_signed, lane_lo):
    w = x.shape[-1]
    partner = jnp.where(lane_lo, pltpu.roll(x, w - 32, 1), pltpu.roll(x, 32, 1))
    return x * cos + partner * sin_signed


def _inproj_kernel(x_ref, xprev_ref, nw_ref, cos_ref, sin_ref, wa_ref, wb_ref, wkv_ref,
                   cw_ref, cb_ref,
                   gla_ref, q_ref, cmp_ref, ks_ref, vs_ref, kw_ref, vw_ref, gate_ref, z_ref,
                   xbc_ref, dt_ref, xe_ref, *, tiles_per_seq):
    tm = x_ref.shape[0]
    half = tm // 2
    halo = xprev_ref.shape[0]
    seq_start = (pl.program_id(0) % tiles_per_seq) == 0
    cw = cw_ref[...]

    def normed(x):
        y = x * lax.rsqrt(jnp.mean(x * x, axis=-1, keepdims=True) + NORM_EPS) * nw_ref[...]
        return y.astype(BF16)

    for part, lo in enumerate((0, half)):
        rows = slice(lo, lo + half)
        u = normed(x_ref[rows, :])
        if part == 0:
            pb = _dot(jnp.concatenate([normed(xprev_ref[...]), u], axis=0), wb_ref[...])
            before = jnp.where(seq_start, 0.0, pb[0:halo, 512:1536])
            pb = pb[halo:, :]
        else:
            pb = _dot(u, wb_ref[...])
            before = raw_tail
        raw = pb[:, 512:1536]
        raw_tail = raw[half - halo:, :]
        z_ref[rows, :] = pb[:, 0:512]
        dt_ref[rows, :] = pb[:, 1536:1664]
        xe_ref[part, 0:halo, :] = before
        xe_ref[part, halo:halo + half, :] = raw
        conv = jnp.broadcast_to(cb_ref[...], (half, SSD_CONV_DIM))
        for w in range(SSD_CONV):
            conv = conv + xe_ref[part, pl.ds(halo - (SSD_CONV - 1) + w, half), :] * cw[w:w + 1, :]
        xbc_ref[rows, :] = conv * _sigmoid(conv)

        cos = cos_ref[rows, :]
        sin = sin_ref[rows, :]
        lane = lax.broadcasted_iota(jnp.int32, cos.shape, 1)
        lane_lo = (lane & 63) < 32
        q = pb[:, 1664:1920]
        for c in range(2):
            qc = _rope_apply(q[:, c * 128:(c + 1) * 128], cos, sin, lane_lo)
            q_ref[rows, c * 128:(c + 1) * 128] = (qc * Q_SCALE).astype(BF16)
        kv = _dot(u, wkv_ref[...])
        cmp_ref[rows, 0:128] = _rope_apply(kv[:, 0:128], cos, sin, lane_lo)
        cmp_ref[rows, 128:256] = kv[:, 128:256]
        ksl = _rope_apply(kv[:, 256:384], cos, sin, lane_lo)
        pos = ((pl.program_id(0) % tiles_per_seq) * tm + lo
               + lax.broadcasted_iota(jnp.int32, (half, 1), 0))
        flag = jnp.where((lane - 64) == ((pos >> 6) & (NSA_FLAGS - 1)), BIG, 0.0)
        ks_ref[0, rows, :] = jnp.where(lane < 64, ksl, flag).astype(BF16)
        ks_ref[1, rows, :] = jnp.where(lane < 64, pltpu.roll(ksl, 64, 1), flag).astype(BF16)
        kw_ref[rows, :] = _rope_apply(kv[:, 512:640], cos, sin, lane_lo).astype(BF16)
        one = jnp.where(lane == 64, 1.0, 0.0)
        for v_ref, vv in ((vs_ref, kv[:, 384:512]), (vw_ref, kv[:, 640:768])):
            v_ref[0, rows, :] = jnp.where(lane < 64, vv, one).astype(BF16)
            v_ref[1, rows, :] = jnp.where(lane < 64, pltpu.roll(vv, 64, 1), one).astype(BF16)
        pa = _dot(u, wa_ref[...])
        gla_ref[rows, :] = pa[:, 0:896]
        gate_ref[rows, :] = pa[:, 896:1024]


_PROJ_OUT = (("gla", 896, F32), ("q", 256, BF16), ("cmp", 256, F32), ("ks", 128, BF16),
             ("vs", 128, BF16), ("kw", 128, BF16), ("vw", 128, BF16), ("gate", 128, F32),
             ("z", 512, F32), ("xbc", 1024, F32), ("dt", 128, F32))
_PROJ_PER_GROUP = ("ks", "vs", "vw")
_PROJ_W = ("a", "b", "kvall")


def _inproj(h2d, nw, cos, sin, w, cw, cb, layer, seq):
    m = h2d.shape[0]
    tm = TM_PROJ
    nt = seq // tm
    halo = 16
    row = lambda i: (i, 0)
    fixed = lambda i: (0, 0)

    def ospec(name, wd):
        if name in _PROJ_PER_GROUP:
            return pl.BlockSpec((2, tm, wd), lambda i: (0, i, 0))
        return pl.BlockSpec((tm, wd), row)

    def oshape(name, wd, dt):
        return jax.ShapeDtypeStruct((2, m, wd) if name in _PROJ_PER_GROUP else (m, wd), dt)

    outs = pl.pallas_call(
        functools.partial(_inproj_kernel, tiles_per_seq=nt),
        grid=(m // tm,),
        in_specs=[pl.BlockSpec((tm, D_MODEL), row),
                  pl.BlockSpec((halo, D_MODEL),
                               lambda i: (jnp.maximum(i * (tm // halo) - 1, 0), 0)),
                  pl.BlockSpec((1, D_MODEL), fixed),
                  pl.BlockSpec((tm, LANES), lambda i: (i % nt, 0)),
                  pl.BlockSpec((tm, LANES), lambda i: (i % nt, 0))]
                 + [pl.BlockSpec((None,) + w[n].shape[1:], lambda i: (layer, 0, 0))
                    for n in _PROJ_W]
                 + [pl.BlockSpec((SSD_CONV, SSD_CONV_DIM), fixed),
                    pl.BlockSpec((1, SSD_CONV_DIM), fixed)],
        out_specs=[ospec(n, wd) for n, wd, _ in _PROJ_OUT],
        out_shape=[oshape(n, wd, dt) for n, wd, dt in _PROJ_OUT],
        scratch_shapes=[pltpu.VMEM((2, tm // 2 + halo, SSD_CONV_DIM), F32)],
        compiler_params=_cparams(("parallel",)),
        name="inproj",
    )(h2d, h2d, nw, cos, sin, *[w[n] for n in _PROJ_W], cw, cb)
    return {n: o for (n, _, _), o in zip(_PROJ_OUT, outs)}


def _gla_kernel(x_ref, w2_ref, bg_ref, nw_ref, tri_ref, hm_ref, hm2_ref, hmean_ref,
                o_ref, st_ref):
    @pl.when(pl.program_id(1) == 0)
    def _():
        st_ref[...] = jnp.zeros_like(st_ref)

    for s in range(x_ref.shape[0]):
        _gla_tile(s, x_ref, w2_ref, bg_ref, nw_ref, tri_ref, hm_ref, hm2_ref, hmean_ref,
                  o_ref, st_ref)


def _gla_tile(s, x_ref, w2_ref, bg_ref, nw_ref, tri_ref, hm_ref, hm2_ref, hmean_ref,
              o_ref, st_ref):
    tt = x_ref.shape[1]
    nsub = tt // GLA_SUB
    x = x_ref[s]
    q = x[:, 0:128] * (GLA_DK ** -0.5)
    k = x[:, 128:256]
    v = x[:, 256:512]
    r = x[:, 512:768]
    glr = x[:, 768:896].astype(BF16)
    pre = _dot(glr, w2_ref[...]) + bg_ref[...]
    log_a = -_softplus(-pre) * (1.0 / GLA_TAU)
    bc = _split_dot_rhs(tri_ref[...], log_a, 3)

    q3 = q.reshape(nsub, GLA_SUB, 128)
    k3 = k.reshape(nsub, GLA_SUB, 128)
    bc3 = bc.reshape(nsub, GLA_SUB, 128)
    v3 = v.reshape(nsub, GLA_SUB, 256)
    row = lax.broadcasted_iota(jnp.int32, (nsub, GLA_SUB, 128), 1)
    hm = hm_ref[...]

    o = jnp.zeros((tt, 256), F32)
    for j in range(GLA_SUB):
        diff = bc3 - bc3[:, j:j + 1, :]
        w = jnp.exp(jnp.where(row >= j, diff, NEG))
        t = (q3 * k3[:, j:j + 1, :] * w).reshape(tt, 128).astype(BF16)
        a = _dot(t, hm).reshape(nsub, GLA_SUB, 256)
        o = o + (a * v3[:, j:j + 1, :]).reshape(tt, 256)

    gtot3 = bc3[:, GLA_SUB - 1:GLA_SUB, :]
    kdec = (k3 * jnp.exp(gtot3 - bc3)).reshape(tt, 128).astype(BF16)
    qdec = q * jnp.exp(bc)
    bends = [bc[GLA_SUB - 1:GLA_SUB, :]]
    for c in range(1, nsub):
        bends.append(bends[-1] + bc[(c + 1) * GLA_SUB - 1:(c + 1) * GLA_SUB, :])
    bstart = jnp.concatenate(
        [jnp.zeros((GLA_SUB, 128), F32)]
        + [jnp.broadcast_to(bends[c], (GLA_SUB, 128)) for c in range(nsub - 1)], axis=0)
    st0 = st_ref[s]
    o = o + _dot_nt((qdec * jnp.exp(bstart)).astype(BF16), st0.astype(BF16))

    vt = v.T.astype(BF16)
    col = lax.broadcasted_iota(jnp.int32, (256, tt), 1)
    rowi = lax.broadcasted_iota(jnp.int32, (tt, 128), 0)
    hm2 = hm2_ref[...]
    st = st0 * jnp.exp(bends[nsub - 1])
    for b in range(nsub):
        lo = b * GLA_SUB
        vsel = jnp.where(col >= lo, jnp.where(col < lo + GLA_SUB, vt, 0.0), 0.0).astype(BF16)
        upd = _dot(vsel, kdec) * hm2
        if b < nsub - 1:
            later = jnp.exp(jnp.where(rowi >= lo + GLA_SUB, bstart - bends[b], NEG))
            o = o + _dot_nt((qdec * later).astype(BF16), upd.astype(BF16))
            st = st + upd * jnp.exp(bends[nsub - 1] - bends[b])
        else:
            st = st + upd
    st_ref[s] = st

    ms = _split_dot_lhs(o * o, hmean_ref[...], 2)
    y = o * lax.rsqrt(ms + NORM_EPS) * nw_ref[...]
    o_ref[s] = (y * (r * _sigmoid(r))).astype(o_ref.dtype)


def _gla(gla_in, w2, bg, nw, consts):
    b, s, _ = gla_in.shape
    tt = TT_GLA
    nb = next(n for n in (4, 2, 1) if b % n == 0)
    fixed = lambda bi, i: (0, 0)
    return pl.pallas_call(
        _gla_kernel,
        grid=(b // nb, s // tt),
        in_specs=[pl.BlockSpec((nb, tt, 896), lambda bi, i: (bi, i, 0)),
                  pl.BlockSpec((128, 128), fixed),
                  pl.BlockSpec((1, 128), fixed),
                  pl.BlockSpec((1, 256), fixed),
                  pl.BlockSpec((tt, tt), fixed),
                  pl.BlockSpec((128, 256), fixed),
                  pl.BlockSpec((256, 128), fixed),
                  pl.BlockSpec((256, 256), fixed)],
        out_specs=pl.BlockSpec((nb, tt, 256), lambda bi, i: (bi, i, 0)),
        out_shape=jax.ShapeDtypeStruct((b, s, 256), BF16),
        scratch_shapes=[pltpu.VMEM((nb, 256, 128), F32)],
        compiler_params=_cparams(("parallel", "arbitrary")),
        name="gla",
    )(gla_in, w2, bg, nw, consts["gla_tri"], consts["gla_hm"], consts["gla_hm2"],
      consts["gla_hmean"])


def _ssd_kernel(z_ref, x_ref, dt_ref, dtb_ref, alog_ref,
                dskip_ref, nw_ref, tri_ref, ex_ref, o_ref, h_ref):
    @pl.when(pl.program_id(1) == 0)
    def _():
        h_ref[...] = jnp.zeros_like(h_ref)

    for s in range(x_ref.shape[0]):
        _ssd_chunk(s, z_ref, x_ref, dt_ref, dtb_ref, alog_ref, dskip_ref, nw_ref, tri_ref, ex_ref,
                   o_ref, h_ref)


def _ssd_chunk(s, z_ref, x_ref, dt_ref, dtb_ref, alog_ref, dskip_ref, nw_ref, tri_ref, ex_ref,
               o_ref, h_ref):
    L = x_ref.shape[1]
    xbc = x_ref[s]
    xs = xbc[:, 0:512]
    bm = xbc[:, 512:768]
    cm = xbc[:, 768:1024]

    dt = _softplus(dt_ref[s] + dtb_ref[...])
    a = dt * (-jnp.exp(alog_ref[...]))
    acs = _split_dot_rhs(tri_ref[...], a, 3)
    ex = ex_ref[...]
    dt_e = _split_dot_lhs(dt, ex, 3)
    acs_e = _split_dot_lhs(acs, ex, 3)
    acs_last_e = acs_e[L - 1:L, :]
    xdt = xs * dt_e
    xdt_b = xdt.astype(BF16)
    xd = (xdt * jnp.exp(acs_last_e - acs_e)).astype(BF16)
    eacs_e = jnp.exp(acs_e)
    cdec_e = jnp.exp(acs_last_e)
    acs_t = acs.T
    ri = lax.broadcasted_iota(jnp.int32, (L, L), 0)
    ci = lax.broadcasted_iota(jnp.int32, (L, L), 1)
    causal = ri >= ci

    ys = []
    for g in range(2):
        bg = bm[:, g * 128:(g + 1) * 128]
        cg = cm[:, g * 128:(g + 1) * 128].astype(BF16)
        cb = _dot_nt(cg, bg.astype(BF16))
        hg = h_ref[s, :, g * 256:(g + 1) * 256]
        yoff = _dot(cg, hg.astype(BF16)) * eacs_e[:, g * 256:(g + 1) * 256]
        ydiag = []
        for hh in range(4):
            hd = g * 4 + hh
            seg = jnp.exp(jnp.where(causal, acs[:, hd:hd + 1] - acs_t[hd:hd + 1, :], NEG))
            ydiag.append(_dot((cb * seg).astype(BF16), xdt_b[:, hd * 64:(hd + 1) * 64]))
        ys.append(yoff + jnp.concatenate(ydiag, axis=1))
        st = _dot(bg.T.astype(BF16), xd[:, g * 256:(g + 1) * 256])
        h_ref[s, :, g * 256:(g + 1) * 256] = hg * cdec_e[:, g * 256:(g + 1) * 256] + st

    y = jnp.concatenate(ys, axis=1) + dskip_ref[...] * xs
    zz = z_ref[s]
    y = y * (zz * _sigmoid(zz))
    nw = nw_ref[...]
    for g in range(2):
        yg = y[:, g * 256:(g + 1) * 256]
        ms = jnp.mean(yg * yg, axis=-1, keepdims=True)
        o_ref[s, :, g * 256:(g + 1) * 256] = (
            yg * lax.rsqrt(ms + NORM_EPS) * nw[:, g * 256:(g + 1) * 256]).astype(o_ref.dtype)


def _ssd(z, xbc, dt, dtb, alog, dskip_e, nw, consts):
    b, s, _ = z.shape
    L = L_SSD
    nb = next(n for n in (4, 2, 1) if b % n == 0)
    fixed = lambda bi, i: (0, 0)
    tok = lambda bi, i: (bi, i, 0)
    return pl.pallas_call(
        _ssd_kernel,
        grid=(b // nb, s // L),
        in_specs=[pl.BlockSpec((nb, L, 512), tok),
                  pl.BlockSpec((nb, L, 1024), tok),
                  pl.BlockSpec((nb, L, 128), tok),
                  pl.BlockSpec((1, 128), fixed),
                  pl.BlockSpec((1, 128), fixed),
                  pl.BlockSpec((1, 512), fixed),
                  pl.BlockSpec((1, 512), fixed),
                  pl.BlockSpec((L, L), fixed),
                  pl.BlockSpec((128, 512), fixed)],
        out_specs=pl.BlockSpec((nb, L, 512), tok),
        out_shape=jax.ShapeDtypeStruct((b, s, 512), BF16),
        scratch_shapes=[pltpu.VMEM((nb, 128, 512), F32)],
        compiler_params=_cparams(("parallel", "arbitrary")),
        name="ssd",
    )(z, xbc, dt, dtb, alog, dskip_e, nw, consts["ssd_tri"], consts["ssd_ex"])


def _cmp_kernel(x_ref, pos_ref, w1_ref, w2_ref, o_ref):
    nsb = o_ref.shape[2]
    pos = pos_ref[0]
    lo = jnp.zeros((nsb, 256), F32)
    hi = jnp.zeros((nsb, 256), F32)
    for t in range(NSA_CMP_STRIDE):
        xt = x_ref[0, pl.ds(t, nsb, stride=NSA_CMP_STRIDE), :]
        lo = lo + _dot((xt + pos[t:t + 1, :]).astype(BF16), w1_ref[0, t])
        u = t + NSA_CMP_STRIDE
        hi = hi + _dot((xt + pos[u:u + 1, :]).astype(BF16), w1_ref[0, u])
    hid = lo + pltpu.roll(hi, nsb - 1, 0)
    gl = 0.5 * hid * (1.0 + jnp.tanh(0.7978845608028654 * (hid + 0.044715 * hid * hid * hid)))
    o_ref[0, 0] = _dot(gl.astype(BF16), w2_ref[0]).astype(o_ref.dtype)


def _compress(x, pos, w1, w2):
    b, s, _ = x.shape
    nsb = s // NSA_CMP_STRIDE
    return pl.pallas_call(
        _cmp_kernel,
        grid=(b, 2),
        in_specs=[pl.BlockSpec((1, s, 128), lambda bi, c: (bi, 0, c)),
                  pl.BlockSpec((1, NSA_CMP_LEN, 128), lambda bi, c: (c, 0, 0)),
                  pl.BlockSpec((1, NSA_CMP_LEN, 128, 256), lambda bi, c: (c, 0, 0, 0)),
                  pl.BlockSpec((1, 256, 128), lambda bi, c: (c, 0, 0))],
        out_specs=pl.BlockSpec((1, 1, nsb, 128), lambda bi, c: (c, bi, 0, 0)),
        out_shape=jax.ShapeDtypeStruct((2, b, nsb, 128), BF16),
        compiler_params=_cparams(("parallel", "parallel")),
        name="nsa_compress",
    )(x, pos, w1, w2)


def _block_diag2(w):
    z = jnp.zeros_like(w)
    return jnp.concatenate([jnp.concatenate([w, z], axis=-1),
                            jnp.concatenate([z, w], axis=-1)], axis=-2)


def _nsa_kernel(q_ref, kc_ref, vc_ref, ks_ref, vs_ref, kw_ref, vw_ref, gate_ref, nw_ref, ov_ref,
                o_ref):
    tq = q_ref.shape[1]
    nsb = kc_ref.shape[1]
    tk = TK_NSA
    t0 = pl.program_id(1) * tq
    groups = (0, 1)

    lane = lax.broadcasted_iota(jnp.int32, (tq, LANES), 1)
    lane_lo = lane < 64
    gmask = (lane_lo, lane >= 64)
    tpos = t0 + lax.broadcasted_iota(jnp.int32, (tq, 1), 0)

    q2, qs = [], []
    for g in groups:
        qa = q_ref[0, :, g * 128:(g + 1) * 128].astype(F32)
        qb = pltpu.roll(qa, 64, 1)
        h0, h1 = (qa, qb) if g == 0 else (qb, qa)
        q2.append(jnp.concatenate([jnp.where(gmask[g], h0, 0.0), jnp.where(gmask[g], h1, 0.0)],
                                  axis=0).astype(BF16))
        qs.append((jnp.where(lane_lo, qa, 0.0), jnp.where(lane_lo, qb, 0.0)))

    nidx = lax.broadcasted_iota(jnp.int32, (tq, nsb), 1)
    cvalid = (nidx * NSA_CMP_STRIDE + (NSA_CMP_LEN - 1)) <= tpos

    def select(ncw, nb):
        o_cmp, imp_ts = [], []
        for g in groups:
            s = _dot_nt(q2[g], kc_ref[0, 0:ncw, :]).reshape(2, tq, ncw)
            s = jnp.where(cvalid[None, :, 0:ncw], s, NEG)
            e = jnp.where(cvalid[None, :, 0:ncw],
                          jnp.exp2(s - jnp.max(s, axis=-1, keepdims=True)), 0.0)
            den = jnp.sum(e, axis=-1, keepdims=True)
            p = e / jnp.where(den > 0.0, den, 1.0)
            oc = _dot(p.reshape(2 * tq, ncw).astype(BF16), vc_ref[0, 0:ncw, :])
            if g == 1:
                oc = pltpu.roll(oc, 64, 1)
            o_cmp.append(oc.reshape(2, tq, LANES))
            imp = _split_dot_lhs(p[0] + p[1], ov_ref[0:ncw, :], 2)
            imp_ts.append(imp.T[0:nb, :])

        imp_t = jnp.concatenate(imp_ts, axis=1)
        blk = lax.broadcasted_iota(jnp.int32, (nb, 2 * tq), 0)
        tcol = lax.broadcasted_iota(jnp.int32, (1, 2 * tq), 1)
        qblk = (t0 + jnp.where(tcol >= tq, tcol - tq, tcol)) >> 6
        forced = (blk == 0) | ((blk <= qblk) & (blk > qblk - NSA_SEL_LOCAL))
        future = blk > qblk
        blk_f = blk.astype(F32)

        def pick(_, work):
            mx = jnp.max(work, axis=0, keepdims=True)
            first = jnp.min(jnp.where(work == mx, blk_f, 1e9), axis=0, keepdims=True)
            return jnp.where(blk_f == first, -3e38, work)

        n_forced = 1 + NSA_SEL_LOCAL
        work = lax.fori_loop(0, NSA_SEL_TOPK - n_forced, pick,
                             jnp.where(forced, -3e38, jnp.where(future, NEG, imp_t)))
        veto_t = jnp.where(work < -2e38, jnp.where(future, -1.0, 0.0), -1.0)
        if nb < LANES:
            veto_t = jnp.concatenate([veto_t, jnp.full((LANES - nb, 2 * tq), -1.0, F32)], axis=0)
        return (o_cmp[0], o_cmp[1], veto_t[:, 0:tq].T, veto_t[:, tq:2 * tq].T)

    nclass = max(1, min(4, nsb // LANES))
    cw, bw = nsb // nclass, LANES // nclass
    size_class = jnp.minimum((t0 + tq - NSA_CMP_LEN) // (cw * NSA_CMP_STRIDE), nclass - 1)
    oc0, oc1, veto0, veto1 = lax.switch(
        size_class, [functools.partial(select, cw * (i + 1), bw * (i + 1)) for i in range(nclass)])
    o_cmp, veto = (oc0, oc1), (veto0, veto1)

    flag_lane = (lane >= 64) & (lane < 64 + NSA_FLAGS)

    def slc_step(c, width, carry, diagonal=False):
        k0 = pl.multiple_of(c * tk, tk)
        shift = (64 - NSA_FLAGS * c) & 127
        if diagonal:
            causal = (lax.broadcasted_iota(jnp.int32, (tq, width), 1)
                      <= lax.broadcasted_iota(jnp.int32, (tq, width), 0) + (width - tq))
        out = []
        for g in groups:
            m, acc = carry[g]
            ks = ks_ref[g, 0, pl.ds(k0, width), :]
            vs = vs_ref[g, 0, pl.ds(k0, width), :]
            aug = jnp.where(flag_lane, pltpu.roll(veto[g], shift, 1), 0.0)
            qq = jnp.concatenate([qs[g][0] + aug, qs[g][1] + aug], axis=0).astype(BF16)
            sc = _dot_nt(qq, ks).reshape(2, tq, width)
            if diagonal:
                sc = jnp.where(causal[None], sc, NEG)
            m_new = jnp.maximum(m, jnp.max(sc, axis=-1, keepdims=True))
            pe = jnp.exp2(sc - m_new).reshape(2 * tq, width).astype(BF16)
            pv = _dot(pe, vs).reshape(2, tq, LANES)
            out.append((m_new, jnp.exp2(m - m_new) * acc + pv))
        return tuple(out)

    init = (jnp.full((2, tq, 1), NEG, F32), jnp.zeros((2, tq, LANES), F32))
    nfull = t0 // tk
    carry = lax.fori_loop(
        0, nfull // 2, lambda c, cr: slc_step(2 * c + 1, tk, slc_step(2 * c, tk, cr)),
        (init, init))
    carry = lax.cond(nfull % 2 == 1, lambda cr: slc_step(nfull - 1, tk, cr), lambda cr: cr, carry)
    carry = lax.switch(
        (t0 - nfull * tk) // tq,
        [functools.partial(slc_step, nfull, (j + 1) * tq, diagonal=True) for j in range(tk // tq)],
        carry)

    lw = tq + NSA_WINDOW
    w0 = pl.multiple_of(jnp.maximum(t0 - NSA_WINDOW, 0), tq)
    kw = kw_ref[0, pl.ds(w0, lw), :]
    dist = tpos - (w0 + lax.broadcasted_iota(jnp.int32, (tq, lw), 1))
    wvalid = jnp.where(dist >= 0, dist, NSA_WINDOW) < NSA_WINDOW

    sg = _sigmoid(gate_ref[0])
    for g in groups:
        acc_s = carry[g][1]
        o_slc = acc_s / acc_s[:, :, 64:65]
        sw = jnp.where(wvalid[None], _dot_nt(q2[g], kw).reshape(2, tq, lw), NEG)
        ew = jnp.exp2(sw - jnp.max(sw, axis=-1, keepdims=True))
        ow = _dot(ew.reshape(2 * tq, lw).astype(BF16), vw_ref[g, 0, pl.ds(w0, lw), :])
        ow = ow.reshape(2, tq, LANES)
        o_win = ow / ow[:, :, 64:65]
        outs = []
        for r in range(2):
            gcol = [jnp.sum(jnp.where(lane == g * 6 + r * 3 + c, sg, 0.0), axis=-1, keepdims=True)
                    for c in range(3)]
            o = gcol[0] * o_cmp[g][r] + gcol[1] * o_slc[r] + gcol[2] * o_win[r]
            ms = jnp.sum(jnp.where(lane_lo, o * o, 0.0), axis=-1, keepdims=True) * (1.0 / NSA_DH)
            outs.append(o * lax.rsqrt(ms + NORM_EPS))
        o_ref[0, :, g * 128:(g + 1) * 128] = (
            jnp.where(lane_lo, outs[0], pltpu.roll(outs[1], 64, 1))
            * nw_ref[:, g * 128:(g + 1) * 128]).astype(o_ref.dtype)


def _nsa(q, cmp, ks, vs, kw, vw, gate, nw, consts):
    b, s, _ = q.shape
    nsb = cmp.shape[2]
    tq = TQ_NSA
    return pl.pallas_call(
        _nsa_kernel,
        grid=(b, s // tq),
        in_specs=[pl.BlockSpec((1, tq, 256), lambda bi, i: (bi, i, 0)),
                  pl.BlockSpec((None, 1, nsb, 128), lambda bi, i: (0, bi, 0, 0)),
                  pl.BlockSpec((None, 1, nsb, 128), lambda bi, i: (1, bi, 0, 0)),
                  pl.BlockSpec((2, 1, s, 128), lambda bi, i: (0, bi, 0, 0)),
                  pl.BlockSpec((2, 1, s, 128), lambda bi, i: (0, bi, 0, 0)),
                  pl.BlockSpec((1, s, 128), lambda bi, i: (bi, 0, 0)),
                  pl.BlockSpec((2, 1, s, 128), lambda bi, i: (0, bi, 0, 0)),
                  pl.BlockSpec((1, tq, 128), lambda bi, i: (bi, i, 0)),
                  pl.BlockSpec((1, 256), lambda bi, i: (0, 0)),
                  pl.BlockSpec((nsb, 128), lambda bi, i: (0, 0))],
        out_specs=pl.BlockSpec((1, tq, 256), lambda bi, i: (bi, i, 0)),
        out_shape=jax.ShapeDtypeStruct((b, s, 256), BF16),
        compiler_params=_cparams(("parallel", "arbitrary")),
        name="nsa_attn",
    )(q, cmp, cmp, ks, vs, kw, vw, gate, nw, consts["nsa_overlap"])


def _outmlp_kernel(h_ref, yg_ref, yn_ref, ys_ref, wog_ref, won_ref, wos_ref, n2_ref,
                   wup_ref, wdn_ref, fn_ref, o_ref, *, final_norm):
    h2 = (h_ref[...] + _dot(yg_ref[...], wog_ref[...]) + _dot(yn_ref[...], won_ref[...])
          + _dot(ys_ref[...], wos_ref[...]))
    u = h2 * lax.rsqrt(jnp.mean(h2 * h2, axis=-1, keepdims=True) + NORM_EPS) * n2_ref[...]
    a = jnp.maximum(_dot(u.astype(BF16), wup_ref[...]), 0.0)
    out = h2 + _dot((a * a).astype(BF16), wdn_ref[...])
    if final_norm:
        out = (out * lax.rsqrt(jnp.mean(out * out, axis=-1, keepdims=True) + NORM_EPS)
               * fn_ref[...])
    o_ref[...] = out


def _outmlp(h2d, yg, yn, ys, wo, n2, wup, wdn, fn, layer, final_norm):
    m = h2d.shape[0]
    tm = TM_MLP
    row = lambda i: (i, 0)
    fixed = lambda i: (0, 0)
    whole = lambda i: (layer, 0, 0)
    return pl.pallas_call(
        functools.partial(_outmlp_kernel, final_norm=final_norm),
        grid=(m // tm,),
        in_specs=[pl.BlockSpec((tm, D_MODEL), row),
                  pl.BlockSpec((tm, 256), row),
                  pl.BlockSpec((tm, 256), row),
                  pl.BlockSpec((tm, 512), row),
                  pl.BlockSpec((None, 256, D_MODEL), lambda i: (layer, 0, 0)),
                  pl.BlockSpec((None, 256, D_MODEL), lambda i: (layer, 1, 0)),
                  pl.BlockSpec((None, 512, D_MODEL), lambda i: (layer, 1, 0)),
                  pl.BlockSpec((1, D_MODEL), fixed),
                  pl.BlockSpec((None, D_MODEL, D_FF), whole),
                  pl.BlockSpec((None, D_FF, D_MODEL), whole),
                  pl.BlockSpec((1, D_MODEL), fixed)],
        out_specs=pl.BlockSpec((tm, D_MODEL), row),
        out_shape=jax.ShapeDtypeStruct((m, D_MODEL), F32),
        compiler_params=_cparams(("parallel",)),
        name="outproj_mlp",
    )(h2d, yg, yn, ys, wo, wo, wo, n2, wup, wdn, fn)


def _constants(seq):
    c = {}
    i = np.arange(TT_GLA)
    c["gla_tri"] = ((i[:, None] >= i[None, :]) & (i[:, None] // GLA_SUB == i[None, :] // GLA_SUB))
    dk = np.arange(128) // GLA_DK
    dv = np.arange(256) // GLA_DV
    c["gla_hm"] = dk[:, None] == dv[None, :]
    c["gla_hm2"] = dv[:, None] == dk[None, :]
    c["gla_hmean"] = (dv[:, None] == dv[None, :]) / float(GLA_DV)
    i = np.arange(L_SSD)
    c["ssd_tri"] = i[:, None] >= i[None, :]
    hd = np.arange(128)
    c["ssd_ex"] = hd[:, None] == (np.arange(512) // 64)[None, :]
    nsb = seq // NSA_CMP_STRIDE
    cs = np.arange(nsb) * NSA_CMP_STRIDE
    ss = np.arange(LANES) * NSA_SEL_BLOCK
    c["nsa_overlap"] = ((cs[:, None] < ss[None, :] + NSA_SEL_BLOCK)
                        & (cs[:, None] + NSA_CMP_LEN > ss[None, :]))
    out = {k: jnp.asarray(np.asarray(v, np.float32), BF16) for k, v in c.items()}
    out["gla_hm2"] = out["gla_hm2"].astype(F32)
    return out


def _rope_tables(seq):
    half = NSA_DH // 2
    inv = ROPE_THETA ** (-jnp.arange(half, dtype=F32) / half)
    ang = jnp.arange(seq).astype(F32)[:, None] * inv[None, :]
    cos, sin = jnp.cos(ang), jnp.sin(ang)
    return jnp.tile(cos, (1, 4)), jnp.tile(jnp.concatenate([-sin, sin], axis=1), (1, 2))


def _pad_cols(w, width):
    return jnp.pad(w, ((0, 0), (0, 0), (0, width - w.shape[2])))


def _proj_weights(w_in):
    gq, gk, gv, glr, gr, nq, nkv, ngate, sz, sxbc, sdt = jnp.split(
        w_in.astype(BF16), np.cumsum(IN_SPLITS)[:-1].tolist(), axis=2)
    return dict(
        a=jnp.concatenate([gq, gk, gv, gr, _pad_cols(glr, 128), _pad_cols(ngate, 128)], axis=2),
        b=jnp.concatenate([sz, sxbc, _pad_cols(sdt, 128), nq], axis=2),
        kvall=nkv)


def kernel(x, norm1_w, w_in, gla_gate_w2, gla_gate_b, gla_norm_w, nsa_cmp_pos_k, nsa_cmp_w1_k, nsa_cmp_w2_k, nsa_cmp_pos_v, nsa_cmp_w1_v, nsa_cmp_w2_v, nsa_norm_w, ssd_conv_w, ssd_conv_b, ssd_dt_bias, ssd_a_log, ssd_d, ssd_norm_w, w_out, norm2_w, w_up, w_down, final_norm_w):
    bsz, seq, _ = x.shape
    depth = w_in.shape[0]
    m = bsz * seq
    nsb = seq // NSA_CMP_STRIDE
    assert seq % 2048 == 0 and seq // NSA_SEL_BLOCK <= LANES
    consts = _constants(seq)
    cos, sin = _rope_tables(seq)
    h = x.reshape(m, D_MODEL)
    w_proj = _proj_weights(w_in)
    wo_b, wup_b, wdn_b = w_out.astype(BF16), w_up.astype(BF16), w_down.astype(BF16)
    for l in range(depth):
        p = _inproj(h, norm1_w[l][None, :], cos, sin, w_proj, ssd_conv_w[l],
                    ssd_conv_b[l][None, :], l, seq)

        y_gla = _gla(p["gla"].reshape(bsz, seq, 896),
                     jnp.pad(gla_gate_w2[l], ((0, 112), (0, 0))).astype(BF16),
                     gla_gate_b[l][None, :], gla_norm_w[l][None, :], consts)

        y_ssd = _ssd(p["z"].reshape(bsz, seq, 512), p["xbc"].reshape(bsz, seq, 1024),
                     p["dt"].reshape(bsz, seq, 128),
                     jnp.pad(ssd_dt_bias[l], (0, 120))[None, :],
                     jnp.pad(ssd_a_log[l], (0, 120))[None, :],
                     jnp.repeat(ssd_d[l], 64)[None, :], ssd_norm_w[l][None, :], consts)

        pos = jnp.tile(jnp.stack([nsa_cmp_pos_k[l], nsa_cmp_pos_v[l]]), (1, 1, 2))
        w1 = jnp.stack([nsa_cmp_w1_k[l], nsa_cmp_w1_v[l]]).astype(BF16)
        w1 = _block_diag2(w1.reshape(2, NSA_CMP_LEN, NSA_DH, 128))
        w2 = _block_diag2(jnp.stack([nsa_cmp_w2_k[l], nsa_cmp_w2_v[l]]).astype(BF16))
        cmp = _compress(p["cmp"].reshape(bsz, seq, 256), pos, w1, w2)

        y_nsa = _nsa(p["q"].reshape(bsz, seq, 256), cmp,
                     p["ks"].reshape(2, bsz, seq, 128), p["vs"].reshape(2, bsz, seq, 128),
                     p["kw"].reshape(bsz, seq, 128), p["vw"].reshape(2, bsz, seq, 128),
                     p["gate"].reshape(bsz, seq, 128),
                     nsa_norm_w[l][None, :], consts)

        h = _outmlp(h, y_gla.reshape(m, 256), y_nsa.reshape(m, 256), y_ssd.reshape(m, 512),
                    wo_b, norm2_w[l][None, :], wup_b, wdn_b, final_norm_w[None, :], l,
                    final_norm=(l == depth - 1))
    return h.reshape(bsz, seq, D_MODEL)
```

```python
import functools

import jax
import jax.numpy as jnp
import numpy as np
from jax import lax
from jax.experimental import pallas as pl
from jax.experimental.pallas import tpu as pltpu

F32 = jnp.float32
BF16 = jnp.bfloat16

D_MODEL = 1024
GLA_DK, GLA_DV = 32, 64
GLA_TAU = 16.0
GLA_SUB = 16
NSA_DH = 64
NSA_CMP_LEN, NSA_CMP_STRIDE = 32, 16
NSA_SEL_BLOCK, NSA_SEL_TOPK, NSA_SEL_LOCAL = 64, 16, 2
NSA_WINDOW = 512
SSD_D_INNER = 512
SSD_CONV = 4
SSD_CONV_DIM = 1024
D_FF = 4 * D_MODEL
ROPE_THETA = 10000.0
NORM_EPS = 1e-6
NEG = -1e30
BIG = 1e30
LANES = 128
VMEM_LIMIT = 56 * 1024 * 1024

IN_SPLITS = (128, 128, 256, 16, 256, 256, 768, 12, 512, SSD_CONV_DIM, 8)

TM_PROJ = 512
TT_GLA = 128
L_SSD = 256
TQ_NSA = 256
TK_NSA = 2048
NSA_FLAGS = TK_NSA // NSA_SEL_BLOCK
Q_SCALE = NSA_DH ** -0.5 * 1.4426950408889634
TM_MLP = 512


def _cparams(sem):
    return pltpu.CompilerParams(dimension_semantics=sem, vmem_limit_bytes=VMEM_LIMIT)


def _dot(a, b):
    return jnp.dot(a, b, preferred_element_type=F32)


def _dot_nt(a, b):
    return lax.dot_general(a, b, (((1,), (1,)), ((), ())), preferred_element_type=F32)


def _split_dot_lhs(a, b_bf16, terms):
    acc, rem = None, a
    for _ in range(terms):
        hi = rem.astype(BF16)
        part = _dot(hi, b_bf16)
        acc = part if acc is None else acc + part
        rem = rem - hi.astype(F32)
    return acc


def _split_dot_rhs(a_bf16, b, terms):
    acc, rem = None, b
    for _ in range(terms):
        hi = rem.astype(BF16)
        part = _dot(a_bf16, hi)
        acc = part if acc is None else acc + part
        rem = rem - hi.astype(F32)
    return acc


def _sigmoid(x):
    return 1.0 / (1.0 + jnp.exp(-x))


def _softplus(x):
    return jnp.maximum(x, 0.0) + jnp.log1p(jnp.exp(-jnp.abs(x)))


def _rope_apply(x, cos, sin_signed, lane_lo):
    w = x.shape[-1]
    partner = jnp.where(lane_lo, pltpu.roll(x, w - 32, 1), pltpu.roll(x, 32, 1))
    return x * cos + partner * sin_signed


def _inproj_kernel(x_ref, xprev_ref, nw_ref, cos_ref, sin_ref, wa_ref, wb_ref, wkv_ref,
                   cw_ref, cb_ref,
                   gla_ref, q_ref, cmp_ref, ks_ref, vs_ref, kw_ref, vw_ref, gate_ref, z_ref,
                   xbc_ref, dt_ref, xe_ref, *, tiles_per_seq):
    tm = x_ref.shape[0]
    half = tm // 2
    halo = xprev_ref.shape[0]
    seq_start = (pl.program_id(0) % tiles_per_seq) == 0
    cw = cw_ref[...]

    def normed(x):
        y = x * lax.rsqrt(jnp.mean(x * x, axis=-1, keepdims=True) + NORM_EPS) * nw_ref[...]
        return y.astype(BF16)

    for part, lo in enumerate((0, half)):
        rows = slice(lo, lo + half)
        u = normed(x_ref[rows, :])
        if part == 0:
            pb = _dot(jnp.concatenate([normed(xprev_ref[...]), u], axis=0), wb_ref[...])
            before = jnp.where(seq_start, 0.0, pb[0:halo, 512:1536])
            pb = pb[halo:, :]
        else:
            pb = _dot(u, wb_ref[...])
            before = raw_tail
        raw = pb[:, 512:1536]
        raw_tail = raw[half - halo:, :]
        z_ref[rows, :] = pb[:, 0:512]
        dt_ref[rows, :] = pb[:, 1536:1664]
        xe_ref[part, 0:halo, :] = before
        xe_ref[part, halo:halo + half, :] = raw
        conv = jnp.broadcast_to(cb_ref[...], (half, SSD_CONV_DIM))
        for w in range(SSD_CONV):
            conv = conv + xe_ref[part, pl.ds(halo - (SSD_CONV - 1) + w, half), :] * cw[w:w + 1, :]
        xbc_ref[rows, :] = conv * _sigmoid(conv)

        cos = cos_ref[rows, :]
        sin = sin_ref[rows, :]
        lane = lax.broadcasted_iota(jnp.int32, cos.shape, 1)
        lane_lo = (lane & 63) < 32
        q = pb[:, 1664:1920]
        for c in range(2):
            qc = _rope_apply(q[:, c * 128:(c + 1) * 128], cos, sin, lane_lo)
            q_ref[rows, c * 128:(c + 1) * 128] = (qc * Q_SCALE).astype(BF16)
        kv = _dot(u, wkv_ref[...])
        cmp_ref[rows, 0:128] = _rope_apply(kv[:, 0:128], cos, sin, lane_lo)
        cmp_ref[rows, 128:256] = kv[:, 128:256]
        ksl = _rope_apply(kv[:, 256:384], cos, sin, lane_lo)
        pos = ((pl.program_id(0) % tiles_per_seq) * tm + lo
               + lax.broadcasted_iota(jnp.int32, (half, 1), 0))
        flag = jnp.where((lane - 64) == ((pos >> 6) & (NSA_FLAGS - 1)), BIG, 0.0)
        ks_ref[0, rows, :] = jnp.where(lane < 64, ksl, flag).astype(BF16)
        ks_ref[1, rows, :] = jnp.where(lane < 64, pltpu.roll(ksl, 64, 1), flag).astype(BF16)
        kw_ref[rows, :] = _rope_apply(kv[:, 512:640], cos, sin, lane_lo).astype(BF16)
        one = jnp.where(lane == 64, 1.0, 0.0)
        for v_ref, vv in ((vs_ref, kv[:, 384:512]), (vw_ref, kv[:, 640:768])):
            v_ref[0, rows, :] = jnp.where(lane < 64, vv, one).astype(BF16)
            v_ref[1, rows, :] = jnp.where(lane < 64, pltpu.roll(vv, 64, 1), one).astype(BF16)
        pa = _dot(u, wa_ref[...])
        gla_ref[rows, :] = pa[:, 0:896]
        gate_ref[rows, :] = pa[:, 896:1024]


_PROJ_OUT = (("gla", 896, F32), ("q", 256, BF16), ("cmp", 256, F32), ("ks", 128, BF16),
             ("vs", 128, BF16), ("kw", 128, BF16), ("vw", 128, BF16), ("gate", 128, F32),
             ("z", 512, F32), ("xbc", 1024, F32), ("dt", 128, F32))
_PROJ_PER_GROUP = ("ks", "vs", "vw")
_PROJ_W = ("a", "b", "kvall")


def _inproj(h2d, nw, cos, sin, w, cw, cb, layer, seq):
    m = h2d.shape[0]
    tm = TM_PROJ
    nt = seq // tm
    halo = 16
    row = lambda i: (i, 0)
    fixed = lambda i: (0, 0)

    def ospec(name, wd):
        if name in _PROJ_PER_GROUP:
            return pl.BlockSpec((2, tm, wd), lambda i: (0, i, 0))
        return pl.BlockSpec((tm, wd), row)

    def oshape(name, wd, dt):
        return jax.ShapeDtypeStruct((2, m, wd) if name in _PROJ_PER_GROUP else (m, wd), dt)

    outs = pl.pallas_call(
        functools.partial(_inproj_kernel, tiles_per_seq=nt),
        grid=(m // tm,),
        in_specs=[pl.BlockSpec((tm, D_MODEL), row),
                  pl.BlockSpec((halo, D_MODEL),
                               lambda i: (jnp.maximum(i * (tm // halo) - 1, 0), 0)),
                  pl.BlockSpec((1, D_MODEL), fixed),
                  pl.BlockSpec((tm, LANES), lambda i: (i % nt, 0)),
                  pl.BlockSpec((tm, LANES), lambda i: (i % nt, 0))]
                 + [pl.BlockSpec((None,) + w[n].shape[1:], lambda i: (layer, 0, 0))
                    for n in _PROJ_W]
                 + [pl.BlockSpec((SSD_CONV, SSD_CONV_DIM), fixed),
                    pl.BlockSpec((1, SSD_CONV_DIM), fixed)],
        out_specs=[ospec(n, wd) for n, wd, _ in _PROJ_OUT],
        out_shape=[oshape(n, wd, dt) for n, wd, dt in _PROJ_OUT],
        scratch_shapes=[pltpu.VMEM((2, tm // 2 + halo, SSD_CONV_DIM), F32)],
        compiler_params=_cparams(("parallel",)),
        name="inproj",
    )(h2d, h2d, nw, cos, sin, *[w[n] for n in _PROJ_W], cw, cb)
    return {n: o for (n, _, _), o in zip(_PROJ_OUT, outs)}


def _gla_kernel(x_ref, w2_ref, bg_ref, nw_ref, tri_ref, hm_ref, hm2_ref, hmean_ref,
                o_ref, st_ref):
    @pl.when(pl.program_id(1) == 0)
    def _():
        st_ref[...] = jnp.zeros_like(st_ref)

    for s in range(x_ref.shape[0]):
        _gla_tile(s, x_ref, w2_ref, bg_ref, nw_ref, tri_ref, hm_ref, hm2_ref, hmean_ref,
                  o_ref, st_ref)


def _gla_tile(s, x_ref, w2_ref, bg_ref, nw_ref, tri_ref, hm_ref, hm2_ref, hmean_ref,
              o_ref, st_ref):
    tt = x_ref.shape[1]
    nsub = tt // GLA_SUB
    x = x_ref[s]
    q = x[:, 0:128] * (GLA_DK ** -0.5)
    k = x[:, 128:256]
    v = x[:, 256:512]
    r = x[:, 512:768]
    glr = x[:, 768:896].astype(BF16)
    pre = _dot(glr, w2_ref[...]) + bg_ref[...]
    log_a = -_softplus(-pre) * (1.0 / GLA_TAU)
    bc = _split_dot_rhs(tri_ref[...], log_a, 3)

    q3 = q.reshape(nsub, GLA_SUB, 128)
    k3 = k.reshape(nsub, GLA_SUB, 128)
    bc3 = bc.reshape(nsub, GLA_SUB, 128)
    v3 = v.reshape(nsub, GLA_SUB, 256)
    row = lax.broadcasted_iota(jnp.int32, (nsub, GLA_SUB, 128), 1)
    hm = hm_ref[...]

    o = jnp.zeros((tt, 256), F32)
    for j in range(GLA_SUB):
        diff = bc3 - bc3[:, j:j + 1, :]
        w = jnp.exp(jnp.where(row >= j, diff, NEG))
        t = (q3 * k3[:, j:j + 1, :] * w).reshape(tt, 128).astype(BF16)
        a = _dot(t, hm).reshape(nsub, GLA_SUB, 256)
        o = o + (a * v3[:, j:j + 1, :]).reshape(tt, 256)

    gtot3 = bc3[:, GLA_SUB - 1:GLA_SUB, :]
    kdec = (k3 * jnp.exp(gtot3 - bc3)).reshape(tt, 128).astype(BF16)
    qdec = q * jnp.exp(bc)
    bends = [bc[GLA_SUB - 1:GLA_SUB, :]]
    for c in range(1, nsub):
        bends.append(bends[-1] + bc[(c + 1) * GLA_SUB - 1:(c + 1) * GLA_SUB, :])
    bstart = jnp.concatenate(
        [jnp.zeros((GLA_SUB, 128), F32)]
        + [jnp.broadcast_to(bends[c], (GLA_SUB, 128)) for c in range(nsub - 1)], axis=0)
    st0 = st_ref[s]
    o = o + _dot_nt((qdec * jnp.exp(bstart)).astype(BF16), st0.astype(BF16))

    vt = v.T.astype(BF16)
    col = lax.broadcasted_iota(jnp.int32, (256, tt), 1)
    rowi = lax.broadcasted_iota(jnp.int32, (tt, 128), 0)
    hm2 = hm2_ref[...]
    st = st0 * jnp.exp(bends[nsub - 1])
    for b in range(nsub):
        lo = b * GLA_SUB
        vsel = jnp.where(col >= lo, jnp.where(col < lo + GLA_SUB, vt, 0.0), 0.0).astype(BF16)
        upd = _dot(vsel, kdec) * hm2
        if b < nsub - 1:
            later = jnp.exp(jnp.where(rowi >= lo + GLA_SUB, bstart - bends[b], NEG))
            o = o + _dot_nt((qdec * later).astype(BF16), upd.astype(BF16))
            st = st + upd * jnp.exp(bends[nsub - 1] - bends[b])
        else:
            st = st + upd
    st_ref[s] = st

    ms = _split_dot_lhs(o * o, hmean_ref[...], 2)
    y = o * lax.rsqrt(ms + NORM_EPS) * nw_ref[...]
    o_ref[s] = (y * (r * _sigmoid(r))).astype(o_ref.dtype)


def _gla(gla_in, w2, bg, nw, consts):
    b, s, _ = gla_in.shape
    tt = TT_GLA
    nb = next(n for n in (4, 2, 1) if b % n == 0)
    fixed = lambda bi, i: (0, 0)
    return pl.pallas_call(
        _gla_kernel,
        grid=(b // nb, s // tt),
        in_specs=[pl.BlockSpec((nb, tt, 896), lambda bi, i: (bi, i, 0)),
                  pl.BlockSpec((128, 128), fixed),
                  pl.BlockSpec((1, 128), fixed),
                  pl.BlockSpec((1, 256), fixed),
                  pl.BlockSpec((tt, tt), fixed),
                  pl.BlockSpec((128, 256), fixed),
                  pl.BlockSpec((256, 128), fixed),
                  pl.BlockSpec((256, 256), fixed)],
        out_specs=pl.BlockSpec((nb, tt, 256), lambda bi, i: (bi, i, 0)),
        out_shape=jax.ShapeDtypeStruct((b, s, 256), BF16),
        scratch_shapes=[pltpu.VMEM((nb, 256, 128), F32)],
        compiler_params=_cparams(("parallel", "arbitrary")),
        name="gla",
    )(gla_in, w2, bg, nw, consts["gla_tri"], consts["gla_hm"], consts["gla_hm2"],
      consts["gla_hmean"])


def _ssd_kernel(z_ref, x_ref, dt_ref, dtb_ref, alog_ref,
                dskip_ref, nw_ref, tri_ref, ex_ref, o_ref, h_ref):
    @pl.when(pl.program_id(1) == 0)
    def _():
        h_ref[...] = jnp.zeros_like(h_ref)

    for s in range(x_ref.shape[0]):
        _ssd_chunk(s, z_ref, x_ref, dt_ref, dtb_ref, alog_ref, dskip_ref, nw_ref, tri_ref, ex_ref,
                   o_ref, h_ref)


def _ssd_chunk(s, z_ref, x_ref, dt_ref, dtb_ref, alog_ref, dskip_ref, nw_ref, tri_ref, ex_ref,
               o_ref, h_ref):
    L = x_ref.shape[1]
    xbc = x_ref[s]
    xs = xbc[:, 0:512]
    bm = xbc[:, 512:768]
    cm = xbc[:, 768:1024]

    dt = _softplus(dt_ref[s] + dtb_ref[...])
    a = dt * (-jnp.exp(alog_ref[...]))
    acs = _split_dot_rhs(tri_ref[...], a, 3)
    ex = ex_ref[...]
    dt_e = _split_dot_lhs(dt, ex, 3)
    acs_e = _split_dot_lhs(acs, ex, 3)
    acs_last_e = acs_e[L - 1:L, :]
    xdt = xs * dt_e
    xdt_b = xdt.astype(BF16)
    xd = (xdt * jnp.exp(acs_last_e - acs_e)).astype(BF16)
    eacs_e = jnp.exp(acs_e)
    cdec_e = jnp.exp(acs_last_e)
    acs_t = acs.T
    ri = lax.broadcasted_iota(jnp.int32, (L, L), 0)
    ci = lax.broadcasted_iota(jnp.int32, (L, L), 1)
    causal = ri >= ci

    ys = []
    for g in range(2):
        bg = bm[:, g * 128:(g + 1) * 128]
        cg = cm[:, g * 128:(g + 1) * 128].astype(BF16)
        cb = _dot_nt(cg, bg.astype(BF16))
        hg = h_ref[s, :, g * 256:(g + 1) * 256]
        yoff = _dot(cg, hg.astype(BF16)) * eacs_e[:, g * 256:(g + 1) * 256]
        ydiag = []
        for hh in range(4):
            hd = g * 4 + hh
            seg = jnp.exp(jnp.where(causal, acs[:, hd:hd + 1] - acs_t[hd:hd + 1, :], NEG))
            ydiag.append(_dot((cb * seg).astype(BF16), xdt_b[:, hd * 64:(hd + 1) * 64]))
        ys.append(yoff + jnp.concatenate(ydiag, axis=1))
        st = _dot(bg.T.astype(BF16), xd[:, g * 256:(g + 1) * 256])
        h_ref[s, :, g * 256:(g + 1) * 256] = hg * cdec_e[:, g * 256:(g + 1) * 256] + st

    y = jnp.concatenate(ys, axis=1) + dskip_ref[...] * xs
    zz = z_ref[s]
    y = y * (zz * _sigmoid(zz))
    nw = nw_ref[...]
    for g in range(2):
        yg = y[:, g * 256:(g + 1) * 256]
        ms = jnp.mean(yg * yg, axis=-1, keepdims=True)
        o_ref[s, :, g * 256:(g + 1) * 256] = (
            yg * lax.rsqrt(ms + NORM_EPS) * nw[:, g * 256:(g + 1) * 256]).astype(o_ref.dtype)


def _ssd(z, xbc, dt, dtb, alog, dskip_e, nw, consts):
    b, s, _ = z.shape
    L = L_SSD
    nb = next(n for n in (4, 2, 1) if b % n == 0)
    fixed = lambda bi, i: (0, 0)
    tok = lambda bi, i: (bi, i, 0)
    return pl.pallas_call(
        _ssd_kernel,
        grid=(b // nb, s // L),
        in_specs=[pl.BlockSpec((nb, L, 512), tok),
                  pl.BlockSpec((nb, L, 1024), tok),
                  pl.BlockSpec((nb, L, 128), tok),
                  pl.BlockSpec((1, 128), fixed),
                  pl.BlockSpec((1, 128), fixed),
                  pl.BlockSpec((1, 512), fixed),
                  pl.BlockSpec((1, 512), fixed),
                  pl.BlockSpec((L, L), fixed),
                  pl.BlockSpec((128, 512), fixed)],
        out_specs=pl.BlockSpec((nb, L, 512), tok),
        out_shape=jax.ShapeDtypeStruct((b, s, 512), BF16),
        scratch_shapes=[pltpu.VMEM((nb, 128, 512), F32)],
        compiler_params=_cparams(("parallel", "arbitrary")),
        name="ssd",
    )(z, xbc, dt, dtb, alog, dskip_e, nw, consts["ssd_tri"], consts["ssd_ex"])


def _cmp_kernel(x_ref, pos_ref, w1_ref, w2_ref, o_ref):
    nsb = o_ref.shape[2]
    pos = pos_ref[0]
    lo = jnp.zeros((nsb, 256), F32)
    hi = jnp.zeros((nsb, 256), F32)
    for t in range(NSA_CMP_STRIDE):
        xt = x_ref[0, pl.ds(t, nsb, stride=NSA_CMP_STRIDE), :]
        lo = lo + _dot((xt + pos[t:t + 1, :]).astype(BF16), w1_ref[0, t])
        u = t + NSA_CMP_STRIDE
        hi = hi + _dot((xt + pos[u:u + 1, :]).astype(BF16), w1_ref[0, u])
    hid = lo + pltpu.roll(hi, nsb - 1, 0)
    gl = 0.5 * hid * (1.0 + jnp.tanh(0.7978845608028654 * (hid + 0.044715 * hid * hid * hid)))
    o_ref[0, 0] = _dot(gl.astype(BF16), w2_ref[0]).astype(o_ref.dtype)


def _compress(x, pos, w1, w2):
    b, s, _ = x.shape
    nsb = s // NSA_CMP_STRIDE
    return pl.pallas_call(
        _cmp_kernel,
        grid=(b, 2),
        in_specs=[pl.BlockSpec((1, s, 128), lambda bi, c: (bi, 0, c)),
                  pl.BlockSpec((1, NSA_CMP_LEN, 128), lambda bi, c: (c, 0, 0)),
                  pl.BlockSpec((1, NSA_CMP_LEN, 128, 256), lambda bi, c: (c, 0, 0, 0)),
                  pl.BlockSpec((1, 256, 128), lambda bi, c: (c, 0, 0))],
        out_specs=pl.BlockSpec((1, 1, nsb, 128), lambda bi, c: (c, bi, 0, 0)),
        out_shape=jax.ShapeDtypeStruct((2, b, nsb, 128), BF16),
        compiler_params=_cparams(("parallel", "parallel")),
        name="nsa_compress",
    )(x, pos, w1, w2)


def _block_diag2(w):
    z = jnp.zeros_like(w)
    return jnp.concatenate([jnp.concatenate([w, z], axis=-1),
                            jnp.concatenate([z, w], axis=-1)], axis=-2)


def _nsa_kernel(q_ref, kc_ref, vc_ref, ks_ref, vs_ref, kw_ref, vw_ref, gate_ref, nw_ref, ov_ref,
                o_ref):
    tq = q_ref.shape[1]
    nsb = kc_ref.shape[1]
    tk = TK_NSA
    t0 = pl.program_id(1) * tq
    groups = (0, 1)

    lane = lax.broadcasted_iota(jnp.int32, (tq, LANES), 1)
    lane_lo = lane < 64
    gmask = (lane_lo, lane >= 64)
    tpos = t0 + lax.broadcasted_iota(jnp.int32, (tq, 1), 0)

    q2, qs = [], []
    for g in groups:
        qa = q_ref[0, :, g * 128:(g + 1) * 128].astype(F32)
        qb = pltpu.roll(qa, 64, 1)
        h0, h1 = (qa, qb) if g == 0 else (qb, qa)
        q2.append(jnp.concatenate([jnp.where(gmask[g], h0, 0.0), jnp.where(gmask[g], h1, 0.0)],
                                  axis=0).astype(BF16))
        qs.append((jnp.where(lane_lo, qa, 0.0), jnp.where(lane_lo, qb, 0.0)))

    nidx = lax.broadcasted_iota(jnp.int32, (tq, nsb), 1)
    cvalid = (nidx * NSA_CMP_STRIDE + (NSA_CMP_LEN - 1)) <= tpos

    def select(ncw, nb):
        o_cmp, imp_ts = [], []
        for g in groups:
            s = _dot_nt(q2[g], kc_ref[0, 0:ncw, :]).reshape(2, tq, ncw)
            s = jnp.where(cvalid[None, :, 0:ncw], s, NEG)
            e = jnp.where(cvalid[None, :, 0:ncw],
                          jnp.exp2(s - jnp.max(s, axis=-1, keepdims=True)), 0.0)
            den = jnp.sum(e, axis=-1, keepdims=True)
            p = e / jnp.where(den > 0.0, den, 1.0)
            oc = _dot(p.reshape(2 * tq, ncw).astype(BF16), vc_ref[0, 0:ncw, :])
            if g == 1:
                oc = pltpu.roll(oc, 64, 1)
            o_cmp.append(oc.reshape(2, tq, LANES))
            imp = _split_dot_lhs(p[0] + p[1], ov_ref[0:ncw, :], 2)
            imp_ts.append(imp.T[0:nb, :])

        imp_t = jnp.concatenate(imp_ts, axis=1)
        blk = lax.broadcasted_iota(jnp.int32, (nb, 2 * tq), 0)
        tcol = lax.broadcasted_iota(jnp.int32, (1, 2 * tq), 1)
        qblk = (t0 + jnp.where(tcol >= tq, tcol - tq, tcol)) >> 6
        forced = (blk == 0) | ((blk <= qblk) & (blk > qblk - NSA_SEL_LOCAL))
        future = blk > qblk
        blk_f = blk.astype(F32)

        def pick(_, work):
            mx = jnp.max(work, axis=0, keepdims=True)
            first = jnp.min(jnp.where(work == mx, blk_f, 1e9), axis=0, keepdims=True)
            return jnp.where(blk_f == first, -3e38, work)

        n_forced = 1 + NSA_SEL_LOCAL
        work = lax.fori_loop(0, NSA_SEL_TOPK - n_forced, pick,
                             jnp.where(forced, -3e38, jnp.where(future, NEG, imp_t)))
        veto_t = jnp.where(work < -2e38, jnp.where(future, -1.0, 0.0), -1.0)
        if nb < LANES:
            veto_t = jnp.concatenate([veto_t, jnp.full((LANES - nb, 2 * tq), -1.0, F32)], axis=0)
        return (o_cmp[0], o_cmp[1], veto_t[:, 0:tq].T, veto_t[:, tq:2 * tq].T)

    nclass = max(1, min(4, nsb // LANES))
    cw, bw = nsb // nclass, LANES // nclass
    size_class = jnp.minimum((t0 + tq - NSA_CMP_LEN) // (cw * NSA_CMP_STRIDE), nclass - 1)
    oc0, oc1, veto0, veto1 = lax.switch(
        size_class, [functools.partial(select, cw * (i + 1), bw * (i + 1)) for i in range(nclass)])
    o_cmp, veto = (oc0, oc1), (veto0, veto1)

    flag_lane = (lane >= 64) & (lane < 64 + NSA_FLAGS)

    def slc_step(c, width, carry, diagonal=False):
        k0 = pl.multiple_of(c * tk, tk)
        shift = (64 - NSA_FLAGS * c) & 127
        if diagonal:
            causal = (lax.broadcasted_iota(jnp.int32, (tq, width), 1)
                      <= lax.broadcasted_iota(jnp.int32, (tq, width), 0) + (width - tq))
        out = []
        for g in groups:
            m, acc = carry[g]
            ks = ks_ref[g, 0, pl.ds(k0, width), :]
            vs = vs_ref[g, 0, pl.ds(k0, width), :]
            aug = jnp.where(flag_lane, pltpu.roll(veto[g], shift, 1), 0.0)
            qq = jnp.concatenate([qs[g][0] + aug, qs[g][1] + aug], axis=0).astype(BF16)
            sc = _dot_nt(qq, ks).reshape(2, tq, width)
            if diagonal:
                sc = jnp.where(causal[None], sc, NEG)
            m_new = jnp.maximum(m, jnp.max(sc, axis=-1, keepdims=True))
            pe = jnp.exp2(sc - m_new).reshape(2 * tq, width).astype(BF16)
            pv = _dot(pe, vs).reshape(2, tq, LANES)
            out.append((m_new, jnp.exp2(m - m_new) * acc + pv))
        return tuple(out)

    init = (jnp.full((2, tq, 1), NEG, F32), jnp.zeros((2, tq, LANES), F32))
    nfull = t0 // tk
    carry = lax.fori_loop(0, nfull, lambda c, cr: slc_step(c, tk, cr), (init, init))
    carry = lax.switch(
        (t0 - nfull * tk) // tq,
        [functools.partial(slc_step, nfull, (j + 1) * tq, diagonal=True) for j in range(tk // tq)],
        carry)

    lw = tq + NSA_WINDOW
    w0 = pl.multiple_of(jnp.maximum(t0 - NSA_WINDOW, 0), tq)
    kw = kw_ref[0, pl.ds(w0, lw), :]
    dist = tpos - (w0 + lax.broadcasted_iota(jnp.int32, (tq, lw), 1))
    wvalid = jnp.where(dist >= 0, dist, NSA_WINDOW) < NSA_WINDOW

    sg = _sigmoid(gate_ref[0])
    for g in groups:
        acc_s = carry[g][1]
        o_slc = acc_s / acc_s[:, :, 64:65]
        sw = jnp.where(wvalid[None], _dot_nt(q2[g], kw).reshape(2, tq, lw), NEG)
        ew = jnp.exp2(sw - jnp.max(sw, axis=-1, keepdims=True))
        ow = _dot(ew.reshape(2 * tq, lw).astype(BF16), vw_ref[g, 0, pl.ds(w0, lw), :])
        ow = ow.reshape(2, tq, LANES)
        o_win = ow / ow[:, :, 64:65]
        outs = []
        for r in range(2):
            gcol = [jnp.sum(jnp.where(lane == g * 6 + r * 3 + c, sg, 0.0), axis=-1, keepdims=True)
                    for c in range(3)]
            o = gcol[0] * o_cmp[g][r] + gcol[1] * o_slc[r] + gcol[2] * o_win[r]
            ms = jnp.sum(jnp.where(lane_lo, o * o, 0.0), axis=-1, keepdims=True) * (1.0 / NSA_DH)
            outs.append(o * lax.rsqrt(ms + NORM_EPS))
        o_ref[0, :, g * 128:(g + 1) * 128] = (
            jnp.where(lane_lo, outs[0], pltpu.roll(outs[1], 64, 1))
            * nw_ref[:, g * 128:(g + 1) * 128]).astype(o_ref.dtype)


def _nsa(q, cmp, ks, vs, kw, vw, gate, nw, consts):
    b, s, _ = q.shape
    nsb = cmp.shape[2]
    tq = TQ_NSA
    return pl.pallas_call(
        _nsa_kernel,
        grid=(b, s // tq),
        in_specs=[pl.BlockSpec((1, tq, 256), lambda bi, i: (bi, i, 0)),
                  pl.BlockSpec((None, 1, nsb, 128), lambda bi, i: (0, bi, 0, 0)),
                  pl.BlockSpec((None, 1, nsb, 128), lambda bi, i: (1, bi, 0, 0)),
                  pl.BlockSpec((2, 1, s, 128), lambda bi, i: (0, bi, 0, 0)),
                  pl.BlockSpec((2, 1, s, 128), lambda bi, i: (0, bi, 0, 0)),
                  pl.BlockSpec((1, s, 128), lambda bi, i: (bi, 0, 0)),
                  pl.BlockSpec((2, 1, s, 128), lambda bi, i: (0, bi, 0, 0)),
                  pl.BlockSpec((1, tq, 128), lambda bi, i: (bi, i, 0)),
                  pl.BlockSpec((1, 256), lambda bi, i: (0, 0)),
                  pl.BlockSpec((nsb, 128), lambda bi, i: (0, 0))],
        out_specs=pl.BlockSpec((1, tq, 256), lambda bi, i: (bi, i, 0)),
        out_shape=jax.ShapeDtypeStruct((b, s, 256), BF16),
        compiler_params=_cparams(("parallel", "arbitrary")),
        name="nsa_attn",
    )(q, cmp, cmp, ks, vs, kw, vw, gate, nw, consts["nsa_overlap"])


def _outmlp_kernel(h_ref, yg_ref, yn_ref, ys_ref, wog_ref, won_ref, wos_ref, n2_ref,
                   wup_ref, wdn_ref, fn_ref, o_ref, *, final_norm):
    h2 = (h_ref[...] + _dot(yg_ref[...], wog_ref[...]) + _dot(yn_ref[...], won_ref[...])
          + _dot(ys_ref[...], wos_ref[...]))
    u = h2 * lax.rsqrt(jnp.mean(h2 * h2, axis=-1, keepdims=True) + NORM_EPS) * n2_ref[...]
    a = jnp.maximum(_dot(u.astype(BF16), wup_ref[...]), 0.0)
    out = h2 + _dot((a * a).astype(BF16), wdn_ref[...])
    if final_norm:
        out = (out * lax.rsqrt(jnp.mean(out * out, axis=-1, keepdims=True) + NORM_EPS)
               * fn_ref[...])
    o_ref[...] = out


def _outmlp(h2d, yg, yn, ys, wo, n2, wup, wdn, fn, layer, final_norm):
    m = h2d.shape[0]
    tm = TM_MLP
    row = lambda i: (i, 0)
    fixed = lambda i: (0, 0)
    whole = lambda i: (layer, 0, 0)
    return pl.pallas_call(
        functools.partial(_outmlp_kernel, final_norm=final_norm),
        grid=(m // tm,),
        in_specs=[pl.BlockSpec((tm, D_MODEL), row),
                  pl.BlockSpec((tm, 256), row),
                  pl.BlockSpec((tm, 256), row),
                  pl.BlockSpec((tm, 512), row),
                  pl.BlockSpec((None, 256, D_MODEL), lambda i: (layer, 0, 0)),
                  pl.BlockSpec((None, 256, D_MODEL), lambda i: (layer, 1, 0)),
                  pl.BlockSpec((None, 512, D_MODEL), lambda i: (layer, 1, 0)),
                  pl.BlockSpec((1, D_MODEL), fixed),
                  pl.BlockSpec((None, D_MODEL, D_FF), whole),
                  pl.BlockSpec((None, D_FF, D_MODEL), whole),
                  pl.BlockSpec((1, D_MODEL), fixed)],
        out_specs=pl.BlockSpec((tm, D_MODEL), row),
        out_shape=jax.ShapeDtypeStruct((m, D_MODEL), F32),
        compiler_params=_cparams(("parallel",)),
        name="outproj_mlp",
    )(h2d, yg, yn, ys, wo, wo, wo, n2, wup, wdn, fn)


def _constants(seq):
    c = {}
    i = np.arange(TT_GLA)
    c["gla_tri"] = ((i[:, None] >= i[None, :]) & (i[:, None] // GLA_SUB == i[None, :] // GLA_SUB))
    dk = np.arange(128) // GLA_DK
    dv = np.arange(256) // GLA_DV
    c["gla_hm"] = dk[:, None] == dv[None, :]
    c["gla_hm2"] = dv[:, None] == dk[None, :]
    c["gla_hmean"] = (dv[:, None] == dv[None, :]) / float(GLA_DV)
    i = np.arange(L_SSD)
    c["ssd_tri"] = i[:, None] >= i[None, :]
    hd = np.arange(128)
    c["ssd_ex"] = hd[:, None] == (np.arange(512) // 64)[None, :]
    nsb = seq // NSA_CMP_STRIDE
    cs = np.arange(nsb) * NSA_CMP_STRIDE
    ss = np.arange(LANES) * NSA_SEL_BLOCK
    c["nsa_overlap"] = ((cs[:, None] < ss[None, :] + NSA_SEL_BLOCK)
                        & (cs[:, None] + NSA_CMP_LEN > ss[None, :]))
    out = {k: jnp.asarray(np.asarray(v, np.float32), BF16) for k, v in c.items()}
    out["gla_hm2"] = out["gla_hm2"].astype(F32)
    return out


def _rope_tables(seq):
    half = NSA_DH // 2
    inv = ROPE_THETA ** (-jnp.arange(half, dtype=F32) / half)
    ang = jnp.arange(seq).astype(F32)[:, None] * inv[None, :]
    cos, sin = jnp.cos(ang), jnp.sin(ang)
    return jnp.tile(cos, (1, 4)), jnp.tile(jnp.concatenate([-sin, sin], axis=1), (1, 2))


def _pad_cols(w, width):
    return jnp.pad(w, ((0, 0), (0, 0), (0, width - w.shape[2])))


def _proj_weights(w_in):
    gq, gk, gv, glr, gr, nq, nkv, ngate, sz, sxbc, sdt = jnp.split(
        w_in.astype(BF16), np.cumsum(IN_SPLITS)[:-1].tolist(), axis=2)
    return dict(
        a=jnp.concatenate([gq, gk, gv, gr, _pad_cols(glr, 128), _pad_cols(ngate, 128)], axis=2),
        b=jnp.concatenate([sz, sxbc, _pad_cols(sdt, 128), nq], axis=2),
        kvall=nkv)


def kernel(x, norm1_w, w_in, gla_gate_w2, gla_gate_b, gla_norm_w, nsa_cmp_pos_k, nsa_cmp_w1_k, nsa_cmp_w2_k, nsa_cmp_pos_v, nsa_cmp_w1_v, nsa_cmp_w2_v, nsa_norm_w, ssd_conv_w, ssd_conv_b, ssd_dt_bias, ssd_a_log, ssd_d, ssd_norm_w, w_out, norm2_w, w_up, w_down, final_norm_w):
    bsz, seq, _ = x.shape
    depth = w_in.shape[0]
    m = bsz * seq
    nsb = seq // NSA_CMP_STRIDE
    assert seq % 2048 == 0 and seq // NSA_SEL_BLOCK <= LANES
    consts = _constants(seq)
    cos, sin = _rope_tables(seq)
    h = x.reshape(m, D_MODEL)
    w_proj = _proj_weights(w_in)
    wo_b, wup_b, wdn_b = w_out.astype(BF16), w_up.astype(BF16), w_down.astype(BF16)
    for l in range(depth):
        p = _inproj(h, norm1_w[l][None, :], cos, sin, w_proj, ssd_conv_w[l],
                    ssd_conv_b[l][None, :], l, seq)

        y_gla = _gla(p["gla"].reshape(bsz, seq, 896),
                     jnp.pad(gla_gate_w2[l], ((0, 112), (0, 0))).astype(BF16),
                     gla_gate_b[l][None, :], gla_norm_w[l][None, :], consts)

        y_ssd = _ssd(p["z"].reshape(bsz, seq, 512), p["xbc"].reshape(bsz, seq, 1024),
                     p["dt"].reshape(bsz, seq, 128),
                     jnp.pad(ssd_dt_bias[l], (0, 120))[None, :],
                     jnp.pad(ssd_a_log[l], (0, 120))[None, :],
                     jnp.repeat(ssd_d[l], 64)[None, :], ssd_norm_w[l][None, :], consts)

        pos = jnp.tile(jnp.stack([nsa_cmp_pos_k[l], nsa_cmp_pos_v[l]]), (1, 1, 2))
        w1 = jnp.stack([nsa_cmp_w1_k[l], nsa_cmp_w1_v[l]]).astype(BF16)
        w1 = _block_diag2(w1.reshape(2, NSA_CMP_LEN, NSA_DH, 128))
        w2 = _block_diag2(jnp.stack([nsa_cmp_w2_k[l], nsa_cmp_w2_v[l]]).astype(BF16))
        cmp = _compress(p["cmp"].reshape(bsz, seq, 256), pos, w1, w2)

        y_nsa = _nsa(p["q"].reshape(bsz, seq, 256), cmp,
                     p["ks"].reshape(2, bsz, seq, 128), p["vs"].reshape(2, bsz, seq, 128),
                     p["kw"].reshape(bsz, seq, 128), p["vw"].reshape(2, bsz, seq, 128),
                     p["gate"].reshape(bsz, seq, 128),
                     nsa_norm_w[l][None, :], consts)

        h = _outmlp(h, y_gla.reshape(m, 256), y_nsa.reshape(m, 256), y_ssd.reshape(m, 512),
                    wo_b, norm2_w[l][None, :], wup_b, wdn_b, final_norm_w[None, :], l,
                    final_norm=(l == depth - 1))
    return h.reshape(bsz, seq, D_MODEL)
```
